```python
import jax, jax.numpy as jnp
from jax import lax
import numpy as np

D_MODEL = 2048
BATCH = 4
SEQ = 8192
DEPTH = 4
DEC_BATCH = 2
DEC_SEQ = 8192
PAST_LEN = 128

HEAD_DIM = 128
N_HEADS = 8
N_KV_HEADS = 2
GQA_GROUP = N_HEADS // N_KV_HEADS
ATTN_W = N_HEADS * HEAD_DIM
KV_W = N_KV_HEADS * HEAD_DIM
WINDOW = 128
BLOCK = 128
N_BUCKETS = 32
MAX_DISTANCE = 128
LRU_W = D_MODEL - ATTN_W
LRU_BLOCKS = 8
LRU_BLK = LRU_W // LRU_BLOCKS
CONV_W = 4
LRU_C = 8.0
MIX_W = ATTN_W + LRU_W
IN_W = ATTN_W + 2 * KV_W + 2 * LRU_W
N_EXPERTS = 16
EC_CAPACITY = 2
D_EXPERT = D_MODEL // 2
EPS = 1e-6
NEG = -1e30

kernel_name = "hymba_swa_rglru_ec_encoder"


def rmsnorm(x, g):
    xf = x.astype(jnp.float32)
    y = xf * lax.rsqrt(jnp.mean(xf * xf, axis=-1, keepdims=True) + EPS)
    return (y * g.astype(jnp.float32)).astype(x.dtype)


def t5_bucket(rel):
    nb = N_BUCKETS // 2
    ret = (rel > 0).astype(np.int32) * nb
    n = np.abs(rel)
    max_exact = nb // 2
    large = max_exact + (np.log(np.maximum(n, 1) / max_exact) / np.log(MAX_DISTANCE / max_exact)
                         * (nb - max_exact)).astype(np.int32)
    large = np.minimum(large, nb - 1)
    return (ret + np.where(n < max_exact, n, large)).astype(np.int32)


def band_windows(t):
    B, S, H, D = t.shape
    nblk = S // BLOCK
    tb = jnp.pad(t, ((0, 0), (BLOCK, BLOCK), (0, 0), (0, 0))).reshape(B, nblk + 2, BLOCK, H, D)
    return jnp.concatenate([tb[:, :-2], tb[:, 1:-1], tb[:, 2:]], axis=2)


def windowed_attention(q, k, v, sink, rel_bias):
    B, S, _ = q.shape
    nblk = S // BLOCK
    qb = q.reshape(B, nblk, BLOCK, N_KV_HEADS, GQA_GROUP, HEAD_DIM).astype(jnp.float32)
    kw = band_windows(k.reshape(B, S, N_KV_HEADS, HEAD_DIM)).astype(jnp.float32)
    vw = band_windows(v.reshape(B, S, N_KV_HEADS, HEAD_DIM)).astype(jnp.float32)
    rel = (np.arange(3 * BLOCK)[None, :] - BLOCK) - np.arange(BLOCK)[:, None]
    bias = rel_bias.astype(jnp.float32)[t5_bucket(rel)]
    bias = bias.transpose(2, 0, 1).reshape(N_KV_HEADS, GQA_GROUP, BLOCK, 3 * BLOCK)
    key_pos = np.arange(nblk)[:, None] * BLOCK - BLOCK + np.arange(3 * BLOCK)[None, :]
    mask = (np.abs(rel) <= WINDOW)[None] & ((key_pos >= 0) & (key_pos < S))[:, None, :]
    s = jnp.einsum('bnqkgd,bnjkd->bnkgqj', qb, kw) * (HEAD_DIM ** -0.5) + bias
    s = jnp.where(mask[None, :, None, None], s, NEG)
    sink_l = sink.astype(jnp.float32).reshape(N_KV_HEADS, GQA_GROUP)[None, None, :, :, None, None]
    m = jnp.maximum(jnp.max(s, axis=-1, keepdims=True), sink_l)
    p = jnp.exp(s - m)
    denom = jnp.sum(p, axis=-1, keepdims=True) + jnp.exp(sink_l - m)
    o = jnp.einsum('bnkgqj,bnjkd->bnqkgd', p / denom, vw)
    return o.reshape(B, S, ATTN_W)


def linear_combine(e1, e2):
    a1, b1 = e1
    a2, b2 = e2
    return a1 * a2, a2 * b1 + b2


def rg_lru_direction(x, conv_w, conv_b, w_a, b_a, w_x, b_x, lam, reverse):
    B, S, W = x.shape
    if reverse:
        xp = jnp.pad(x, ((0, 0), (0, CONV_W - 1), (0, 0)))
        xc = sum(conv_w[j] * xp[:, CONV_W - 1 - j:CONV_W - 1 - j + S] for j in range(CONV_W))
    else:
        xp = jnp.pad(x, ((0, 0), (CONV_W - 1, 0), (0, 0)))
        xc = sum(conv_w[j] * xp[:, j:j + S] for j in range(CONV_W))
    xc = xc + conv_b
    blocks = xc.reshape(B, S, LRU_BLOCKS, LRU_BLK)
    r = jax.nn.sigmoid(jnp.einsum('bsnc,ncd->bsnd', blocks, w_a).reshape(B, S, W) + b_a)
    i = jax.nn.sigmoid(jnp.einsum('bsnc,ncd->bsnd', blocks, w_x).reshape(B, S, W) + b_x)
    log_a = -LRU_C * r * jax.nn.softplus(-lam)
    a = jnp.exp(log_a)
    u = jnp.sqrt(-jnp.expm1(2.0 * log_a)) * (i * xc)
    _, h = lax.associative_scan(linear_combine, (a, u), axis=1, reverse=reverse)
    return h


def ec_ffn(h, w_router, w_gate, w_up, w_down):
    B, S, D = h.shape
    n_tok = B * S
    cap = EC_CAPACITY * n_tok // N_EXPERTS
    t = h.reshape(n_tok, D)
    aff = jax.nn.softmax((t @ w_router).astype(jnp.float32), axis=-1)
    gate, idx = lax.top_k(aff.T, cap)
    xe = t[idx]
    hid = jax.nn.silu(jnp.einsum('ecd,edf->ecf', xe, w_gate)) * jnp.einsum('ecd,edf->ecf', xe, w_up)
    ye = jnp.einsum('ecf,efd->ecd', hid, w_down) * gate[..., None].astype(h.dtype)
    out = jnp.zeros((n_tok, D), h.dtype).at[idx.reshape(-1)].add(ye.reshape(-1, D))
    return out.reshape(B, S, D)


def setup_inputs(seed: int = 0) -> dict:
    key = jax.random.key(seed)
    ks = jax.random.split(key, 24)
    f32 = jnp.float32
    nrm = lambda k, shape, scale: jax.random.normal(k, shape, f32) * scale
    u = jax.random.uniform(ks[12], (DEPTH, 2, LRU_W), f32, minval=0.9, maxval=0.999)
    return {
        "x_prompt": nrm(ks[0], (BATCH, SEQ, D_MODEL), 1.0),
        "x_sample": nrm(ks[1], (DEC_BATCH, DEC_SEQ, D_MODEL), 1.0),
        "rel_bias": nrm(ks[2], (N_BUCKETS, N_HEADS), 0.5),
        "norm1_g": 1.0 + nrm(ks[3], (DEPTH, D_MODEL), 0.02),
        "w_in": nrm(ks[4], (DEPTH, D_MODEL, IN_W), D_MODEL ** -0.5),
        "attn_sink": nrm(ks[5], (DEPTH, N_HEADS), 1.0),
        "conv_w": nrm(ks[6], (DEPTH, 2, CONV_W, LRU_W), CONV_W ** -0.5),
        "conv_b": nrm(ks[7], (DEPTH, 2, LRU_W), 0.01),
        "w_rec_a": nrm(ks[8], (DEPTH, 2, LRU_BLOCKS, LRU_BLK, LRU_BLK), LRU_BLK ** -0.5),
        "b_rec_a": nrm(ks[9], (DEPTH, 2, LRU_W), 0.01),
        "w_rec_x": nrm(ks[10], (DEPTH, 2, LRU_BLOCKS, LRU_BLK, LRU_BLK), LRU_BLK ** -0.5),
        "b_rec_x": nrm(ks[11], (DEPTH, 2, LRU_W), 0.01),
        "lru_lambda": jnp.log(u) - jnp.log1p(-u),
        "grp_g_attn": 1.0 + nrm(ks[13], (DEPTH, ATTN_W), 0.02),
        "grp_g_lru": 1.0 + nrm(ks[14], (DEPTH, LRU_W), 0.02),
        "w_out": nrm(ks[15], (DEPTH, MIX_W, D_MODEL), MIX_W ** -0.5),
        "norm2_g": 1.0 + nrm(ks[16], (DEPTH, D_MODEL), 0.02),
        "w_router": nrm(ks[17], (DEPTH, D_MODEL, N_EXPERTS), D_MODEL ** -0.5),
        "w_e_gate": nrm(ks[18], (DEPTH, N_EXPERTS, D_MODEL, D_EXPERT), D_MODEL ** -0.5),
        "w_e_up": nrm(ks[19], (DEPTH, N_EXPERTS, D_MODEL, D_EXPERT), D_MODEL ** -0.5),
        "w_e_down": nrm(ks[20], (DEPTH, N_EXPERTS, D_EXPERT, D_MODEL), D_EXPERT ** -0.5),
        "final_g": 1.0 + nrm(ks[21], (D_MODEL,), 0.02),
    }


def reference(x_prompt, x_sample, rel_bias, norm1_g, w_in, attn_sink, conv_w, conv_b,
              w_rec_a, b_rec_a, w_rec_x, b_rec_x, lru_lambda, grp_g_attn, grp_g_lru,
              w_out, norm2_g, w_router, w_e_gate, w_e_up, w_e_down, final_g):
    def trunk(x):
        for l in range(DEPTH):
            h = rmsnorm(x, norm1_g[l])
            z = h @ w_in[l]
            o1 = ATTN_W
            o2 = o1 + KV_W
            o3 = o2 + KV_W
            o4 = o3 + LRU_W
            q, k, v = z[..., :o1], z[..., o1:o2], z[..., o2:o3]
            xr, yr = z[..., o3:o4].astype(jnp.float32), z[..., o4:].astype(jnp.float32)
            attn = windowed_attention(q, k, v, attn_sink[l], rel_bias)
            cw = conv_w[l].astype(jnp.float32)
            cb = conv_b[l].astype(jnp.float32)
            wa = w_rec_a[l].astype(jnp.float32)
            ba = b_rec_a[l].astype(jnp.float32)
            wx = w_rec_x[l].astype(jnp.float32)
            bx = b_rec_x[l].astype(jnp.float32)
            lam = lru_lambda[l].astype(jnp.float32)
            hf = (rg_lru_direction(xr, cw[0], cb[0], wa[0], ba[0], wx[0], bx[0], lam[0], False)
                  + rg_lru_direction(xr, cw[1], cb[1], wa[1], ba[1], wx[1], bx[1], lam[1], True))
            lru = hf * jax.nn.gelu(yr)
            mixed = jnp.concatenate([rmsnorm(attn.astype(x.dtype), grp_g_attn[l]),
                                     rmsnorm(lru.astype(x.dtype), grp_g_lru[l])], axis=-1)
            x = x + mixed @ w_out[l]
            x = x + ec_ffn(rmsnorm(x, norm2_g[l]), w_router[l], w_e_gate[l], w_e_up[l], w_e_down[l])
        return rmsnorm(x, final_g)

    y_prompt = trunk(x_prompt)
    y_sample = trunk(x_sample)
    return (y_prompt, y_sample)
```

```python
import functools
import math

import numpy as np
import jax
import jax.numpy as jnp
from jax import lax
from jax.experimental import pallas as pl
from jax.experimental.pallas import tpu as pltpu

F32 = jnp.float32
BF16 = jnp.bfloat16
I32 = jnp.int32

D_MODEL = 2048
HEAD_DIM = 128
N_HEADS = 8
N_KV_HEADS = 2
GQA_GROUP = N_HEADS // N_KV_HEADS
ATTN_W = N_HEADS * HEAD_DIM
KV_W = N_KV_HEADS * HEAD_DIM
QKV_W = ATTN_W + 2 * KV_W
WINDOW = 128
BLOCK = 128
N_BUCKETS = 32
MAX_DISTANCE = 128
LRU_W = D_MODEL - ATTN_W
LRU_BLOCKS = 8
LRU_BLK = LRU_W // LRU_BLOCKS
CONV_W = 4
LRU_C = 8.0
N_EXPERTS = 16
EC_CAPACITY = 2
D_EXPERT = D_MODEL // 2
EPS = 1e-6
NEG = -1e30

LANES = 128
SUBLANES = 8
BF16_ROWS = 16
MXU_DIM = 256
VMEM_LIMIT = 56 * 1024 * 1024

ROW_TILE = 512
ATTN_TQ = 512
LRU_CHUNK = 256
ROUTE_TILE = 256
PIECE = BF16_ROWS
EXPERT_TILE = 512
GATE_COLS = LANES
XE_W = D_MODEL + GATE_COLS


def _params(sem):
    return pltpu.CompilerParams(dimension_semantics=sem, vmem_limit_bytes=VMEM_LIMIT)


def _rms(x, g):
    return x * lax.rsqrt(jnp.mean(x * x, axis=-1, keepdims=True) + EPS) * g


def _inproj_kernel(x_ref, g_ref, w_ref, qkv_ref, xy_ref):
    h = _rms(x_ref[...], g_ref[...]).astype(BF16)
    qkv_ref[...] = jnp.dot(h, w_ref[:, :QKV_W], preferred_element_type=F32).astype(BF16)
    xy_ref[...] = jnp.dot(h, w_ref[:, QKV_W:], preferred_element_type=F32)


def _inproj(x, g, w):
    n = x.shape[0]
    in_w = w.shape[1]
    return pl.pallas_call(
        _inproj_kernel,
        grid=(n // ROW_TILE,),
        in_specs=[
            pl.BlockSpec((ROW_TILE, D_MODEL), lambda i: (i, 0)),
            pl.BlockSpec((1, D_MODEL), lambda i: (0, 0)),
            pl.BlockSpec((D_MODEL, in_w), lambda i: (0, 0), pipeline_mode=pl.Buffered(1)),
        ],
        out_specs=[
            pl.BlockSpec((ROW_TILE, QKV_W), lambda i: (i, 0)),
            pl.BlockSpec((ROW_TILE, 2 * LRU_W), lambda i: (i, 0)),
        ],
        out_shape=[
            jax.ShapeDtypeStruct((n, QKV_W), BF16),
            jax.ShapeDtypeStruct((n, 2 * LRU_W), F32),
        ],
        compiler_params=_params(("arbitrary",)),
    )(x, g, w)


def _attn_kernel(q_ref, kp_ref, km_ref, kn_ref, vp_ref, vm_ref, vn_ref, bias_ref, sink_ref, o_ref, *, seq):
    i = pl.program_id(2)
    kwin = jnp.concatenate([kp_ref[0], km_ref[0], kn_ref[0]], axis=0)
    vwin = jnp.concatenate([vp_ref[0], vm_ref[0], vn_ref[0]], axis=0)
    qi = lax.broadcasted_iota(I32, (BLOCK, 3 * BLOCK), 0)
    ji = lax.broadcasted_iota(I32, (BLOCK, 3 * BLOCK), 1)
    band = jnp.abs(ji - BLOCK - qi) <= WINDOW
    scale = HEAD_DIM ** -0.5
    for sb in range(ATTN_TQ // BLOCK):
        ks = kwin[sb * BLOCK:sb * BLOCK + 3 * BLOCK]
        vs = vwin[sb * BLOCK:sb * BLOCK + 3 * BLOCK]
        kpos = i * ATTN_TQ + (sb - 1) * BLOCK + ji
        valid = band & (kpos >= 0) & (kpos < seq)
        for g in range(GQA_GROUP):
            q = q_ref[0, sb * BLOCK:(sb + 1) * BLOCK, g * HEAD_DIM:(g + 1) * HEAD_DIM]
            s = lax.dot_general(q, ks, (((1,), (1,)), ((), ())), preferred_element_type=F32)
            s = jnp.where(valid, s * scale + bias_ref[g], NEG)
            sk = sink_ref[g][:, :1]
            m = jnp.maximum(jnp.max(s, axis=-1, keepdims=True), sk)
            p = jnp.exp(s - m)
            denom = jnp.sum(p, axis=-1, keepdims=True) + jnp.exp(sk - m)
            o = jnp.dot(p.astype(BF16), vs, preferred_element_type=F32)
            o_ref[0, sb * BLOCK:(sb + 1) * BLOCK, g * HEAD_DIM:(g + 1) * HEAD_DIM] = o / denom


def _attention(qkv, bias_tbl, sink_tbl, nseq, seq):
    qkv3 = qkv.reshape(nseq, seq, QKV_W)
    per = ATTN_TQ // BLOCK
    last = seq // BLOCK - 1
    kcol = ATTN_W // HEAD_DIM
    vcol = (ATTN_W + KV_W) // HEAD_DIM

    def side(col, prev):
        if prev:
            return pl.BlockSpec((1, BLOCK, HEAD_DIM), lambda b, k, i: (b, jnp.maximum(i * per - 1, 0), col + k))
        return pl.BlockSpec((1, BLOCK, HEAD_DIM), lambda b, k, i: (b, jnp.minimum(i * per + per, last), col + k))

    def main(col):
        return pl.BlockSpec((1, ATTN_TQ, HEAD_DIM), lambda b, k, i: (b, i, col + k))

    qw = GQA_GROUP * HEAD_DIM
    return pl.pallas_call(
        functools.partial(_attn_kernel, seq=seq),
        grid=(nseq, N_KV_HEADS, seq // ATTN_TQ),
        in_specs=[
            pl.BlockSpec((1, ATTN_TQ, qw), lambda b, k, i: (b, i, k)),
            side(kcol, True), main(kcol), side(kcol, False),
            side(vcol, True), main(vcol), side(vcol, False),
            pl.BlockSpec((GQA_GROUP, BLOCK, 3 * BLOCK), lambda b, k, i: (k, 0, 0)),
            pl.BlockSpec((GQA_GROUP, 1, LANES), lambda b, k, i: (k, 0, 0)),
        ],
        out_specs=pl.BlockSpec((1, ATTN_TQ, qw), lambda b, k, i: (b, i, k)),
        out_shape=jax.ShapeDtypeStruct((nseq, seq, ATTN_W), F32),
        compiler_params=_params(("arbitrary", "arbitrary", "arbitrary")),
    )(qkv3, qkv3, qkv3, qkv3, qkv3, qkv3, qkv3, bias_tbl, sink_tbl)


def _t5_bucket(rel):
    nb = N_BUCKETS // 2
    ret = (rel > 0).astype(np.int32) * nb
    n = np.abs(rel)
    max_exact = nb // 2
    large = max_exact + (np.log(np.maximum(n, 1) / max_exact) / np.log(MAX_DISTANCE / max_exact)
                         * (nb - max_exact)).astype(np.int32)
    large = np.minimum(large, nb - 1)
    return (ret + np.where(n < max_exact, n, large)).astype(np.int32)


def _bias_table(rel_bias):
    rel = (np.arange(3 * BLOCK)[None, :] - BLOCK) - np.arange(BLOCK)[:, None]
    return rel_bias.astype(F32)[_t5_bucket(rel)].transpose(2, 0, 1)


def _lru_scan(a, u, carry, reverse):
    rows = a.shape[0]
    sub = lax.broadcasted_iota(I32, a.shape, 0) & (SUBLANES - 1)
    for d in (1, 2, 4):
        if reverse:
            keep = sub < SUBLANES - d
            shift = rows - d
        else:
            keep = sub >= d
            shift = d
        a_sh = jnp.where(keep, pltpu.roll(a, shift, 0), 1.0)
        u_sh = jnp.where(keep, pltpu.roll(u, shift, 0), 0.0)
        u = a * u_sh + u
        a = a * a_sh
    nv = rows // SUBLANES
    out = [None] * nv
    order = range(nv - 1, -1, -1) if reverse else range(nv)
    edge = 0 if reverse else SUBLANES - 1
    for v in order:
        av = a[v * SUBLANES:(v + 1) * SUBLANES]
        uv = u[v * SUBLANES:(v + 1) * SUBLANES]
        hv = uv + av * carry
        out[v] = hv
        carry = hv[edge:edge + 1]
    return jnp.concatenate(out, axis=0), carry


def _lru_kernel(xr_ref, yr_ref, cw_ref, cb_ref, wa_ref, wx_ref, ba_ref, bx_ref, lam_ref, o_ref, *, seq):
    nch = seq // LRU_CHUNK
    ext = LRU_CHUNK + SUBLANES

    def gates(xc, d):
        xb = xc.astype(BF16)
        r = jax.nn.sigmoid(jnp.dot(xb, wa_ref[d, 0], preferred_element_type=F32) + ba_ref[d])
        ig = jax.nn.sigmoid(jnp.dot(xb, wx_ref[d, 0], preferred_element_type=F32) + bx_ref[d])
        log_a = -LRU_C * r * jax.nn.softplus(-lam_ref[d])
        a = jnp.exp(log_a)
        th = jnp.tanh(log_a)
        u = jnp.sqrt(-2.0 * th / (1.0 - th)) * (ig * xc)
        return a, u

    def fwd(k, carry):
        t0 = pl.multiple_of(k * LRU_CHUNK, LRU_CHUNK)
        halo = xr_ref[0, pl.ds(pl.multiple_of(jnp.maximum(t0 - SUBLANES, 0), SUBLANES), SUBLANES), :]
        halo = jnp.where(k > 0, halo, 0.0)
        xe = jnp.concatenate([halo, xr_ref[0, pl.ds(t0, LRU_CHUNK), :]], axis=0)
        xc = cb_ref[0]
        for j in range(CONV_W):
            s = SUBLANES - (CONV_W - 1) + j
            xc = xc + cw_ref[0, j:j + 1, :] * pltpu.roll(xe, ext - s, 0)[:LRU_CHUNK]
        a, u = gates(xc, 0)
        h, carry = _lru_scan(a, u, carry, False)
        o_ref[0, pl.ds(t0, LRU_CHUNK), :] = h
        return carry

    def bwd(kk, carry):
        k = nch - 1 - kk
        t0 = pl.multiple_of(k * LRU_CHUNK, LRU_CHUNK)
        nxt = jnp.minimum(t0 + LRU_CHUNK, seq - SUBLANES)
        halo = xr_ref[0, pl.ds(pl.multiple_of(nxt, SUBLANES), SUBLANES), :]
        halo = jnp.where(k < nch - 1, halo, 0.0)
        xe = jnp.concatenate([xr_ref[0, pl.ds(t0, LRU_CHUNK), :], halo], axis=0)
        xc = cb_ref[1]
        for j in range(CONV_W):
            s = CONV_W - 1 - j
            tap = xe[:LRU_CHUNK] if s == 0 else pltpu.roll(xe, ext - s, 0)[:LRU_CHUNK]
            xc = xc + cw_ref[1, j:j + 1, :] * tap
        a, u = gates(xc, 1)
        h, carry = _lru_scan(a, u, carry, True)
        hf = o_ref[0, pl.ds(t0, LRU_CHUNK), :]
        o_ref[0, pl.ds(t0, LRU_CHUNK), :] = (hf + h) * jax.nn.gelu(yr_ref[0, pl.ds(t0, LRU_CHUNK), :])
        return carry

    zero = jnp.zeros((1, LRU_BLK), F32)
    lax.fori_loop(0, nch, fwd, zero)
    lax.fori_loop(0, nch, bwd, zero)


def _lru(xy, cw, cb, wa, wx, ba, bx, lam, nseq, seq):
    xy3 = xy.reshape(nseq, seq, 2 * LRU_W)
    vec = lambda: pl.BlockSpec((2, 1, LRU_BLK), lambda b, c: (0, 0, c))
    return pl.pallas_call(
        functools.partial(_lru_kernel, seq=seq),
        grid=(nseq, LRU_BLOCKS),
        in_specs=[
            pl.BlockSpec((1, seq, LRU_BLK), lambda b, c: (b, 0, c)),
            pl.BlockSpec((1, seq, LRU_BLK), lambda b, c: (b, 0, LRU_BLOCKS + c)),
            pl.BlockSpec((2, CONV_W, LRU_BLK), lambda b, c: (0, 0, c)),
            vec(),
            pl.BlockSpec((2, 1, LRU_BLK, LRU_BLK), lambda b, c: (0, c, 0, 0)),
            pl.BlockSpec((2, 1, LRU_BLK, LRU_BLK), lambda b, c: (0, c, 0, 0)),
            vec(), vec(), vec(),
        ],
        out_specs=pl.BlockSpec((1, seq, LRU_BLK), lambda b, c: (b, 0, c)),
        out_shape=jax.ShapeDtypeStruct((nseq, seq, LRU_W), F32),
        compiler_params=_params(("arbitrary", "arbitrary")),
    )(xy3, xy3, cw, cb.reshape(2, 1, LRU_W), wa, wx, ba.reshape(2, 1, LRU_W),
      bx.reshape(2, 1, LRU_W), lam.reshape(2, 1, LRU_W))


def _outproj_kernel(attn_ref, lru_ref, x_ref, ga_ref, gl_ref, w_ref, g2_ref, wr_ref, x1_ref, h2_ref, aff_ref):
    an = _rms(attn_ref[...], ga_ref[...]).astype(BF16)
    ln = _rms(lru_ref[...], gl_ref[...]).astype(BF16)
    y = jnp.dot(an, w_ref[:ATTN_W], preferred_element_type=F32)
    y = y + jnp.dot(ln, w_ref[ATTN_W:], preferred_element_type=F32)
    x1 = x_ref[...] + y
    x1_ref[...] = x1
    hf = _rms(x1, g2_ref[...])
    h2_ref[...] = hf.astype(BF16)
    logits = lax.dot_general(wr_ref[...], hf, (((1,), (1,)), ((), ())),
                             precision=lax.Precision.HIGHEST, preferred_element_type=F32)
    e = jnp.exp(logits - jnp.max(logits, axis=0, keepdims=True))
    aff_ref[...] = e / jnp.sum(e, axis=0, keepdims=True)


def _outproj(attn, lru, x, ga, gl, w, g2, wr_t):
    n = x.shape[0]
    row = lambda w_: pl.BlockSpec((ROW_TILE, w_), lambda i: (i, 0))
    full = lambda a: pl.BlockSpec(a.shape, lambda i: (0,) * a.ndim)
    return pl.pallas_call(
        _outproj_kernel,
        grid=(n // ROW_TILE,),
        in_specs=[row(ATTN_W), row(LRU_W), row(D_MODEL), full(ga), full(gl),
                  pl.BlockSpec(w.shape, lambda i: (0, 0), pipeline_mode=pl.Buffered(1)),
                  full(g2), full(wr_t)],
        out_specs=[row(D_MODEL), row(D_MODEL), pl.BlockSpec((N_EXPERTS, ROW_TILE), lambda i: (0, i))],
        out_shape=[
            jax.ShapeDtypeStruct((n, D_MODEL), F32),
            jax.ShapeDtypeStruct((n, D_MODEL), BF16),
            jax.ShapeDtypeStruct((N_EXPERTS, n), F32),
        ],
        compiler_params=_params(("arbitrary",)),
    )(attn, lru, x, ga, gl, w, g2, wr_t)


def _threshold_kernel(aff_ref, thr_ref, need_ref, gt_ref, eq_ref, *, groups):
    for gi, (lo, hi, cap) in enumerate(groups):
        def keys():
            return lax.bitcast_convert_type(aff_ref[:, lo:hi], I32)

        def step(it, thr):
            cand = thr | jnp.left_shift(jnp.int32(1), 30 - it)
            cnt = jnp.sum((keys() >= cand).astype(F32), axis=1, keepdims=True)
            return jnp.where(cnt >= cap, cand, thr)

        thr = lax.fori_loop(0, 31, step, jnp.zeros((N_EXPERTS, 1), I32))
        ngt = jnp.sum((keys() > thr).astype(F32), axis=1, keepdims=True)
        thr_ref[gi] = jnp.broadcast_to(thr, (N_EXPERTS, LANES))
        need_ref[gi] = jnp.broadcast_to(cap - ngt, (N_EXPERTS, LANES))
        for t in range(lo // ROUTE_TILE, hi // ROUTE_TILE):
            kt = lax.bitcast_convert_type(aff_ref[:, t * ROUTE_TILE:(t + 1) * ROUTE_TILE], I32)
            gt_ref[t] = jnp.broadcast_to(jnp.sum((kt > thr).astype(F32), axis=1, keepdims=True), (N_EXPERTS, LANES))
            eq_ref[t] = jnp.broadcast_to(jnp.sum((kt == thr).astype(F32), axis=1, keepdims=True), (N_EXPERTS, LANES))


def _thresholds(aff_t, groups):
    n = aff_t.shape[1]
    nt = n // ROUTE_TILE
    ng = len(groups)
    return pl.pallas_call(
        functools.partial(_threshold_kernel, groups=groups),
        out_shape=[
            jax.ShapeDtypeStruct((ng, N_EXPERTS, LANES), I32),
            jax.ShapeDtypeStruct((ng, N_EXPERTS, LANES), F32),
            jax.ShapeDtypeStruct((nt, N_EXPERTS, LANES), F32),
            jax.ShapeDtypeStruct((nt, N_EXPERTS, LANES), F32),
        ],
        compiler_params=pltpu.CompilerParams(vmem_limit_bytes=VMEM_LIMIT),
    )(aff_t)


def _piece_copies(tile, cpad_s, fn):
    for e in range(N_EXPERTS):
        npiece = cpad_s[tile * N_EXPERTS + e] // PIECE

        def body(j, _, e=e):
            fn(e, j)
            return 0

        lax.fori_loop(0, npiece, body, 0)


def _match_chunk(gpos, c):
    tm = gpos.shape[1]
    rid = lax.broadcasted_iota(I32, (MXU_DIM, tm), 0) + c * MXU_DIM
    return [gpos[e:e + 1, :] == rid for e in range(N_EXPERTS)]


def _dispatch_kernel(base_s, cpad_s, off_s, nch_s, tail_s,
                     h2_ref, aff_ref, thr_ref, need_ref, ceq_ref, offv_ref,
                     xe_ref, gpos_ref, stage, zeros, sem, *, tile_group, cap_rows):
    i = pl.program_id(0)
    tm = ROUTE_TILE
    grp = tile_group(i)
    thr = jnp.where(grp == 0, thr_ref[0], thr_ref[1])[:, :1]
    need = jnp.where(grp == 0, need_ref[0], need_ref[1])[:, :1]
    aff = aff_ref[...]
    keys = lax.bitcast_convert_type(aff, I32)
    gt = keys > thr
    eq = keys == thr
    before = (lax.broadcasted_iota(I32, (tm, tm), 0) < lax.broadcasted_iota(I32, (tm, tm), 1))
    before = jnp.where(before, 1.0, 0.0).astype(BF16)
    eq_rank = jnp.dot(jnp.where(eq, 1.0, 0.0).astype(BF16), before, preferred_element_type=F32)
    sel = gt | (eq & (ceq_ref[0][:, :1] + eq_rank < need))
    rank = jnp.dot(jnp.where(sel, 1.0, 0.0).astype(BF16), before, preferred_element_type=F32)
    gpos = jnp.where(sel, offv_ref[0][:, :1] + rank.astype(I32), -1)
    gpos_ref[...] = gpos

    def chunk(c, _):
        hit = _match_chunk(gpos, c)
        any_hit = hit[0]
        gate = jnp.where(hit[0], aff[0:1, :], 0.0)
        for e in range(1, N_EXPERTS):
            any_hit = any_hit | hit[e]
            gate = jnp.where(hit[e], aff[e:e + 1, :], gate)
        onehot = jnp.where(any_hit, 1.0, 0.0).astype(BF16)
        rows = jnp.dot(onehot, h2_ref[...], preferred_element_type=F32)
        gv = jnp.sum(gate, axis=1, keepdims=True)
        g_hi = gv.astype(BF16).astype(F32)
        g_mid = (gv - g_hi).astype(BF16).astype(F32)
        g_lo = (gv - g_hi - g_mid).astype(BF16).astype(F32)
        lane = lax.broadcasted_iota(I32, (MXU_DIM, GATE_COLS), 1)
        gcols = jnp.where(lane == 0, g_hi, jnp.where(lane == 1, g_mid, jnp.where(lane == 2, g_lo, 0.0)))
        r0 = pl.multiple_of(c * MXU_DIM, MXU_DIM)
        stage[pl.ds(r0, MXU_DIM), pl.ds(0, D_MODEL)] = rows.astype(BF16)
        stage[pl.ds(r0, MXU_DIM), pl.ds(D_MODEL, GATE_COLS)] = gcols.astype(BF16)
        return 0

    lax.fori_loop(0, nch_s[i], chunk, 0)

    def copy(e, j):
        src = pl.multiple_of(off_s[i * N_EXPERTS + e] + j * PIECE, PIECE)
        dst = pl.multiple_of(e * cap_rows + base_s[i * N_EXPERTS + e] + j * PIECE, PIECE)
        return pltpu.make_async_copy(stage.at[pl.ds(src, PIECE)], xe_ref.at[pl.ds(dst, PIECE)], sem)

    _piece_copies(i, cpad_s, lambda e, j: copy(e, j).start())
    _piece_copies(i, cpad_s, lambda e, j: copy(e, j).wait())

    @pl.when(i == pl.num_programs(0) - 1)
    def _():
        zeros[...] = jnp.zeros(zeros.shape, BF16)
        for e in range(N_EXPERTS):
            def fill(e=e):
                def mk(j):
                    dst = pl.multiple_of(e * cap_rows + tail_s[e] + j * PIECE, PIECE)
                    return pltpu.make_async_copy(zeros, xe_ref.at[pl.ds(dst, PIECE)], sem)
                ntail = tail_s[N_EXPERTS + e]

                def go(j, _):
                    mk(j).start()
                    return 0

                def done(j, _):
                    mk(j).wait()
                    return 0

                lax.fori_loop(0, ntail, go, 0)
                lax.fori_loop(0, ntail, done, 0)
            fill()


def _dispatch(meta, h2, aff_t, thr, need, tile_group, cap_rows):
    n = h2.shape[0]
    nt = n // ROUTE_TILE
    tile3 = lambda: pl.BlockSpec((1, N_EXPERTS, LANES), lambda i, *_: (i, 0, 0))
    full3 = lambda a: pl.BlockSpec(a.shape, lambda i, *_: (0, 0, 0))
    grid_spec = pltpu.PrefetchScalarGridSpec(
        num_scalar_prefetch=5,
        grid=(nt,),
        in_specs=[
            pl.BlockSpec((ROUTE_TILE, D_MODEL), lambda i, *_: (i, 0)),
            pl.BlockSpec((N_EXPERTS, ROUTE_TILE), lambda i, *_: (0, i)),
            full3(thr), full3(need), tile3(), tile3(),
        ],
        out_specs=[
            pl.BlockSpec(memory_space=pl.ANY),
            pl.BlockSpec((N_EXPERTS, ROUTE_TILE), lambda i, *_: (0, i)),
        ],
        scratch_shapes=[
            pltpu.VMEM((N_EXPERTS * ROUTE_TILE, XE_W), BF16),
            pltpu.VMEM((PIECE, XE_W), BF16),
            pltpu.SemaphoreType.DMA,
        ],
    )
    return pl.pallas_call(
        functools.partial(_dispatch_kernel, tile_group=tile_group, cap_rows=cap_rows),
        grid_spec=grid_spec,
        out_shape=[
            jax.ShapeDtypeStruct((N_EXPERTS * cap_rows, XE_W), BF16),
            jax.ShapeDtypeStruct((N_EXPERTS, n), I32),
        ],
        compiler_params=_params(("arbitrary",)),
    )(meta["base"], meta["cpad"], meta["off"], meta["nch"], meta["tail"],
      h2, aff_t, thr, need, meta["ceq_v"], meta["off_v"])


def _expert_kernel(used_s, xe_ref, wg_ref, wu_ref, wd_ref, ye_ref):
    e = pl.program_id(0)
    r = pl.program_id(1)

    @pl.when(r * EXPERT_TILE < used_s[e])
    def _():
        x = xe_ref[:, :D_MODEL]
        gc = xe_ref[:, D_MODEL:].astype(F32)
        gate = gc[:, 0:1] + gc[:, 1:2] + gc[:, 2:3]
        hg = jnp.dot(x, wg_ref[0], preferred_element_type=F32)
        hu = jnp.dot(x, wu_ref[0], preferred_element_type=F32)
        hid = (jax.nn.silu(hg) * hu).astype(BF16)
        y = jnp.dot(hid, wd_ref[0], preferred_element_type=F32) * gate
        ye_ref[...] = y.astype(BF16)


def _experts(used, xe, wg, wu, wd, cap_rows):
    rt = cap_rows // EXPERT_TILE

    def rows(e, r, used_s):
        last = jnp.maximum((used_s[e] + EXPERT_TILE - 1) // EXPERT_TILE - 1, 0)
        return (e * rt + jnp.minimum(r, last), 0)

    grid_spec = pltpu.PrefetchScalarGridSpec(
        num_scalar_prefetch=1,
        grid=(N_EXPERTS, rt),
        in_specs=[
            pl.BlockSpec((EXPERT_TILE, XE_W), rows),
            pl.BlockSpec((1, D_MODEL, D_EXPERT), lambda e, r, u: (e, 0, 0)),
            pl.BlockSpec((1, D_MODEL, D_EXPERT), lambda e, r, u: (e, 0, 0)),
            pl.BlockSpec((1, D_EXPERT, D_MODEL), lambda e, r, u: (e, 0, 0)),
        ],
        out_specs=pl.BlockSpec((EXPERT_TILE, D_MODEL), rows),
    )
    return pl.pallas_call(
        _expert_kernel,
        grid_spec=grid_spec,
        out_shape=jax.ShapeDtypeStruct((N_EXPERTS * cap_rows, D_MODEL), BF16),
        compiler_params=_params(("arbitrary", "arbitrary")),
    )(used, xe, wg, wu, wd)


def _combine_kernel(base_s, cpad_s, off_s, nch_s, x1_ref, gpos_ref, ye_ref, x2_ref, stage, sem, *, cap_rows):
    i = pl.program_id(0)

    @pl.when(i == 0)
    def _():
        stage[...] = jnp.zeros(stage.shape, BF16)

    def copy(e, j):
        src = pl.multiple_of(e * cap_rows + base_s[i * N_EXPERTS + e] + j * PIECE, PIECE)
        dst = pl.multiple_of(off_s[i * N_EXPERTS + e] + j * PIECE, PIECE)
        return pltpu.make_async_copy(ye_ref.at[pl.ds(src, PIECE)], stage.at[pl.ds(dst, PIECE)], sem)

    _piece_copies(i, cpad_s, lambda e, j: copy(e, j).start())
    x2_ref[...] = x1_ref[...]
    gpos = gpos_ref[...]
    _piece_copies(i, cpad_s, lambda e, j: copy(e, j).wait())

    def chunk(c, _):
        hit = _match_chunk(gpos, c)
        any_hit = hit[0]
        for e in range(1, N_EXPERTS):
            any_hit = any_hit | hit[e]
        onehot_t = jnp.where(any_hit, 1.0, 0.0).T.astype(BF16)
        r0 = pl.multiple_of(c * MXU_DIM, MXU_DIM)
        x2_ref[...] += jnp.dot(onehot_t, stage[pl.ds(r0, MXU_DIM), :], preferred_element_type=F32)
        return 0

    lax.fori_loop(0, nch_s[i], chunk, 0)


def _combine(meta, x1, gpos, ye, cap_rows):
    n = x1.shape[0]
    grid_spec = pltpu.PrefetchScalarGridSpec(
        num_scalar_prefetch=4,
        grid=(n // ROUTE_TILE,),
        in_specs=[
            pl.BlockSpec((ROUTE_TILE, D_MODEL), lambda i, *_: (i, 0)),
            pl.BlockSpec((N_EXPERTS, ROUTE_TILE), lambda i, *_: (0, i)),
            pl.BlockSpec(memory_space=pl.ANY),
        ],
        out_specs=pl.BlockSpec((ROUTE_TILE, D_MODEL), lambda i, *_: (i, 0)),
        scratch_shapes=[
            pltpu.VMEM((N_EXPERTS * ROUTE_TILE, D_MODEL), BF16),
            pltpu.SemaphoreType.DMA,
        ],
    )
    return pl.pallas_call(
        functools.partial(_combine_kernel, cap_rows=cap_rows),
        grid_spec=grid_spec,
        out_shape=jax.ShapeDtypeStruct((n, D_MODEL), F32),
        compiler_params=_params(("arbitrary",)),
    )(meta["base"], meta["cpad"], meta["off"], meta["nch"], x1, gpos, ye)


def _final_kernel(x_ref, g_ref, o_ref):
    o_ref[...] = _rms(x_ref[...], g_ref[...])


def _final_norm(x, g):
    n = x.shape[0]
    return pl.pallas_call(
        _final_kernel,
        grid=(n // ROW_TILE,),
        in_specs=[pl.BlockSpec((ROW_TILE, D_MODEL), lambda i: (i, 0)),
                  pl.BlockSpec((1, D_MODEL), lambda i: (0, 0))],
        out_specs=pl.BlockSpec((ROW_TILE, D_MODEL), lambda i: (i, 0)),
        out_shape=jax.ShapeDtypeStruct((n, D_MODEL), F32),
        compiler_params=_params(("arbitrary",)),
    )(x, g)


def _route_meta(gtc, eqc, need, tile_grp, group_first_tile, cap_rows):
    gtc = gtc[:, :, 0].astype(I32)
    eqc = eqc[:, :, 0].astype(I32)
    need_t = need[:, :, 0].astype(I32)[tile_grp]
    ceq = jnp.cumsum(eqc, axis=0) - eqc
    ceq = ceq - ceq[group_first_tile]
    cnt = gtc + jnp.clip(need_t - ceq, 0, eqc)
    cpad = (cnt + PIECE - 1) // PIECE * PIECE
    base = jnp.cumsum(cpad, axis=0) - cpad
    off = jnp.cumsum(cpad, axis=1) - cpad
    used = jnp.sum(cpad, axis=0)
    nch = (jnp.sum(cpad, axis=1) + MXU_DIM - 1) // MXU_DIM
    tail_n = ((used + EXPERT_TILE - 1) // EXPERT_TILE * EXPERT_TILE - used) // PIECE
    bcast = lambda a, dt: jnp.broadcast_to(a[:, :, None].astype(dt), a.shape + (LANES,))
    return {
        "base": base.reshape(-1), "cpad": cpad.reshape(-1), "off": off.reshape(-1), "nch": nch,
        "used": used, "tail": jnp.concatenate([used, tail_n]),
        "ceq_v": bcast(ceq, F32), "off_v": bcast(off, I32),
    }


def kernel(x_prompt, x_sample, rel_bias, norm1_g, w_in, attn_sink, conv_w, conv_b, w_rec_a, b_rec_a,
           w_rec_x, b_rec_x, lru_lambda, grp_g_attn, grp_g_lru, w_out, norm2_g, w_router,
           w_e_gate, w_e_up, w_e_down, final_g):
    bp, seq, _ = x_prompt.shape
    bs = x_sample.shape[0]
    assert x_sample.shape[1] == seq
    nseq = bp + bs
    n_p, n_s = bp * seq, bs * seq
    n = n_p + n_s
    assert seq % ATTN_TQ == 0 and seq % LRU_CHUNK == 0 and n % ROW_TILE == 0
    assert n_p % ROUTE_TILE == 0 and n_s % ROUTE_TILE == 0
    depth = w_in.shape[0]

    groups = ((0, n_p, EC_CAPACITY * n_p // N_EXPERTS), (n_p, n, EC_CAPACITY * n_s // N_EXPERTS))
    nt = n // ROUTE_TILE
    nt_p = n_p // ROUTE_TILE
    tile_grp = np.where(np.arange(nt) < nt_p, 0, 1)
    group_first_tile = np.where(np.arange(nt) < nt_p, 0, nt_p)
    real_rows = groups[0][2] + groups[1][2]
    cap_rows = -(-(real_rows + (PIECE - 1) * nt) // EXPERT_TILE) * EXPERT_TILE
    tile_group = lambda i: (i >= nt_p).astype(I32)

    x = jnp.concatenate([x_prompt.reshape(n_p, D_MODEL), x_sample.reshape(n_s, D_MODEL)], axis=0)
    bias_tbl = _bias_table(rel_bias)
    row = lambda v: v.reshape(1, -1).astype(F32)

    for l in range(depth):
        qkv, xy = _inproj(x, row(norm1_g[l]), w_in[l].astype(BF16))
        sink_tbl = jnp.broadcast_to(attn_sink[l].astype(F32)[:, None, None], (N_HEADS, 1, LANES))
        attn = _attention(qkv, bias_tbl, sink_tbl, nseq, seq).reshape(n, ATTN_W)
        lru = _lru(xy, conv_w[l].astype(F32), conv_b[l].astype(F32), w_rec_a[l].astype(BF16),
                   w_rec_x[l].astype(BF16), b_rec_a[l].astype(F32), b_rec_x[l].astype(F32),
                   lru_lambda[l].astype(F32), nseq, seq).reshape(n, LRU_W)
        x1, h2, aff_t = _outproj(attn, lru, x, row(grp_g_attn[l]), row(grp_g_lru[l]),
                                 w_out[l].astype(BF16), row(norm2_g[l]), w_router[l].astype(F32).T)
        thr, need, gtc, eqc = _thresholds(aff_t, groups)
        meta = _route_meta(gtc, eqc, need, tile_grp, group_first_tile, cap_rows)
        xe, gpos = _dispatch(meta, h2, aff_t, thr, need, tile_group, cap_rows)
        ye = _experts(meta["used"], xe, w_e_gate[l].astype(BF16), w_e_up[l].astype(BF16),
                      w_e_down[l].astype(BF16), cap_rows)
        x = _combine(meta, x1, gpos, ye, cap_rows)

    y = _final_norm(x, row(final_g))
    return (y[:n_p].reshape(bp, seq, D_MODEL), y[n_p:].reshape(bs, seq, D_MODEL))
```

```python
import functools

import numpy as np
import jax
import jax.numpy as jnp
from jax import lax
from jax.experimental import pallas as pl
from jax.experimental.pallas import tpu as pltpu

F32 = jnp.float32
BF16 = jnp.bfloat16
I32 = jnp.int32

D_MODEL = 2048
HEAD_DIM = 128
N_HEADS = 8
N_KV_HEADS = 2
GQA_GROUP = N_HEADS // N_KV_HEADS
ATTN_W = N_HEADS * HEAD_DIM
KV_W = N_KV_HEADS * HEAD_DIM
QKV_W = ATTN_W + 2 * KV_W
WINDOW = 128
BLOCK = 128
N_BUCKETS = 32
MAX_DISTANCE = 128
LRU_W = D_MODEL - ATTN_W
LRU_BLOCKS = 8
LRU_BLK = LRU_W // LRU_BLOCKS
CONV_W = 4
LRU_C = 8.0
N_EXPERTS = 16
EC_CAPACITY = 2
D_EXPERT = D_MODEL // 2
EPS = 1e-6
NEG = -1e30

LANES = 128
SUBLANES = 8
BF16_ROWS = 16
MXU_DIM = 256
VMEM_LIMIT = 56 * 1024 * 1024

ROW_TILE = 512
ATTN_TQ = 512
LRU_SEG = 60
ROUTE_TILE = 256
PIECE = BF16_ROWS
EXPERT_TILE = 512
GATE_COLS = LANES
XE_W = D_MODEL + GATE_COLS


def _params(sem, **kw):
    return pltpu.CompilerParams(dimension_semantics=sem, vmem_limit_bytes=VMEM_LIMIT, **kw)


def _rms(x, g):
    return x * lax.rsqrt(jnp.mean(x * x, axis=-1, keepdims=True) + EPS) * g


def _inproj_kernel(x_ref, g_ref, w_ref, qkv_ref, xy_ref):
    h = _rms(x_ref[...], g_ref[...]).astype(BF16)
    qkv_ref[...] = jnp.dot(h, w_ref[:, :QKV_W], preferred_element_type=F32).astype(BF16)
    xy_ref[...] = jnp.dot(h, w_ref[:, QKV_W:], preferred_element_type=F32)


def _inproj(x, g, w):
    n = x.shape[0]
    in_w = w.shape[1]
    return pl.pallas_call(
        _inproj_kernel,
        grid=(n // ROW_TILE,),
        in_specs=[
            pl.BlockSpec((ROW_TILE, D_MODEL), lambda i: (i, 0)),
            pl.BlockSpec((1, D_MODEL), lambda i: (0, 0)),
            pl.BlockSpec((D_MODEL, in_w), lambda i: (0, 0), pipeline_mode=pl.Buffered(1)),
        ],
        out_specs=[
            pl.BlockSpec((ROW_TILE, QKV_W), lambda i: (i, 0)),
            pl.BlockSpec((ROW_TILE, 2 * LRU_W), lambda i: (i, 0)),
        ],
        out_shape=[
            jax.ShapeDtypeStruct((n, QKV_W), BF16),
            jax.ShapeDtypeStruct((n, 2 * LRU_W), F32),
        ],
        compiler_params=_params(("arbitrary",)),
    )(x, g, w)


def _inproj_first_kernel(xp_ref, xs_ref, g_ref, w_ref, qkv_ref, xy_ref, x_ref, *, tiles_p):
    i = pl.program_id(0)

    def run(src):
        x_ref[...] = src[...]
        _inproj_kernel(src, g_ref, w_ref, qkv_ref, xy_ref)

    pl.when(i < tiles_p)(lambda: run(xp_ref))
    pl.when(i >= tiles_p)(lambda: run(xs_ref))


def _inproj_first(xp, xs, g, w):
    n_p, n_s = xp.shape[0], xs.shape[0]
    n = n_p + n_s
    tile = ROW_TILE // 2
    tiles_p = n_p // tile
    in_w = w.shape[1]
    row = lambda w_: pl.BlockSpec((tile, w_), lambda i: (i, 0))
    return pl.pallas_call(
        functools.partial(_inproj_first_kernel, tiles_p=tiles_p),
        grid=(n // tile,),
        in_specs=[
            pl.BlockSpec((tile, D_MODEL), lambda i: (jnp.minimum(i, tiles_p - 1), 0)),
            pl.BlockSpec((tile, D_MODEL), lambda i: (jnp.maximum(i - tiles_p, 0), 0)),
            pl.BlockSpec((1, D_MODEL), lambda i: (0, 0)),
            pl.BlockSpec((D_MODEL, in_w), lambda i: (0, 0), pipeline_mode=pl.Buffered(1)),
        ],
        out_specs=[row(QKV_W), row(2 * LRU_W), row(D_MODEL)],
        out_shape=[
            jax.ShapeDtypeStruct((n, QKV_W), BF16),
            jax.ShapeDtypeStruct((n, 2 * LRU_W), F32),
            jax.ShapeDtypeStruct((n, D_MODEL), F32),
        ],
        compiler_params=_params(("arbitrary",)),
    )(xp, xs, g, w)


def _attn_kernel(q_ref, kp_ref, km_ref, kn_ref, vp_ref, vm_ref, vn_ref, bias_ref, sink_ref, o_ref, *, seq):
    i = pl.program_id(2)
    kwin = jnp.concatenate([kp_ref[0], km_ref[0], kn_ref[0]], axis=0)
    vwin = jnp.concatenate([vp_ref[0], vm_ref[0], vn_ref[0]], axis=0)
    qi = lax.broadcasted_iota(I32, (BLOCK, 3 * BLOCK), 0)
    ji = lax.broadcasted_iota(I32, (BLOCK, 3 * BLOCK), 1)
    band = jnp.abs(ji - BLOCK - qi) <= WINDOW
    scale = HEAD_DIM ** -0.5
    for sb in range(ATTN_TQ // BLOCK):
        ks = kwin[sb * BLOCK:sb * BLOCK + 3 * BLOCK]
        vs = vwin[sb * BLOCK:sb * BLOCK + 3 * BLOCK]
        kpos = i * ATTN_TQ + (sb - 1) * BLOCK + ji
        valid = band & (kpos >= 0) & (kpos < seq)
        for g in range(GQA_GROUP):
            q = q_ref[0, sb * BLOCK:(sb + 1) * BLOCK, g * HEAD_DIM:(g + 1) * HEAD_DIM]
            s = lax.dot_general(q, ks, (((1,), (1,)), ((), ())), preferred_element_type=F32)
            s = jnp.where(valid, s * scale + bias_ref[g], NEG)
            sk = sink_ref[g][:, :1]
            m = jnp.maximum(jnp.max(s, axis=-1, keepdims=True), sk)
            p = jnp.exp(s - m)
            denom = jnp.sum(p, axis=-1, keepdims=True) + jnp.exp(sk - m)
            o = jnp.dot(p.astype(BF16), vs, preferred_element_type=F32)
            o_ref[0, sb * BLOCK:(sb + 1) * BLOCK, g * HEAD_DIM:(g + 1) * HEAD_DIM] = o / denom


def _attention(qkv, bias_tbl, sink_tbl, nseq, seq):
    qkv3 = qkv.reshape(nseq, seq, QKV_W)
    per = ATTN_TQ // BLOCK
    last = seq // BLOCK - 1
    kcol = ATTN_W // HEAD_DIM
    vcol = (ATTN_W + KV_W) // HEAD_DIM

    def side(col, prev):
        if prev:
            return pl.BlockSpec((1, BLOCK, HEAD_DIM), lambda b, k, i: (b, jnp.maximum(i * per - 1, 0), col + k))
        return pl.BlockSpec((1, BLOCK, HEAD_DIM), lambda b, k, i: (b, jnp.minimum(i * per + per, last), col + k))

    def main(col):
        return pl.BlockSpec((1, ATTN_TQ, HEAD_DIM), lambda b, k, i: (b, i, col + k))

    qw = GQA_GROUP * HEAD_DIM
    return pl.pallas_call(
        functools.partial(_attn_kernel, seq=seq),
        grid=(nseq, N_KV_HEADS, seq // ATTN_TQ),
        in_specs=[
            pl.BlockSpec((1, ATTN_TQ, qw), lambda b, k, i: (b, i, k)),
            side(kcol, True), main(kcol), side(kcol, False),
            side(vcol, True), main(vcol), side(vcol, False),
            pl.BlockSpec((GQA_GROUP, BLOCK, 3 * BLOCK), lambda b, k, i: (k, 0, 0)),
            pl.BlockSpec((GQA_GROUP, 1, LANES), lambda b, k, i: (k, 0, 0)),
        ],
        out_specs=pl.BlockSpec((1, ATTN_TQ, qw), lambda b, k, i: (b, i, k)),
        out_shape=jax.ShapeDtypeStruct((nseq, seq, ATTN_W), F32),
        compiler_params=_params(("arbitrary", "arbitrary", "arbitrary")),
    )(qkv3, qkv3, qkv3, qkv3, qkv3, qkv3, qkv3, bias_tbl, sink_tbl)


def _t5_bucket(rel):
    nb = N_BUCKETS // 2
    ret = (rel > 0).astype(np.int32) * nb
    n = np.abs(rel)
    max_exact = nb // 2
    large = max_exact + (np.log(np.maximum(n, 1) / max_exact) / np.log(MAX_DISTANCE / max_exact)
                         * (nb - max_exact)).astype(np.int32)
    large = np.minimum(large, nb - 1)
    return (ret + np.where(n < max_exact, n, large)).astype(np.int32)


def _bias_table(rel_bias):
    rel = (np.arange(3 * BLOCK)[None, :] - BLOCK) - np.arange(BLOCK)[:, None]
    onehot = np.eye(N_BUCKETS, dtype=np.float32)[_t5_bucket(rel)]
    return jnp.einsum("qkb,bh->hqk", onehot, rel_bias.astype(F32), precision=lax.Precision.HIGHEST)


def _sublane_scan(a, u, reverse):
    sub = lax.broadcasted_iota(I32, a.shape, 0)
    for d in (1, 2, 4):
        keep = (sub < SUBLANES - d) if reverse else (sub >= d)
        shift = SUBLANES - d if reverse else d
        a_sh = jnp.where(keep, pltpu.roll(a, shift, 0), 1.0)
        u_sh = jnp.where(keep, pltpu.roll(u, shift, 0), 0.0)
        u = a * u_sh + u
        a = a * a_sh
    return a, u


def _lru_chunk(xr_ref, base, seg, d, prm, state):
    cw, cb, w, ba, bx, decay = prm
    halo, carry = state[:3], state[3]
    sub = lax.broadcasted_iota(I32, (SUBLANES, LRU_BLK), 0)
    x = [xr_ref[0, pl.ds(base + j, SUBLANES, stride=seg), :] for j in range(seg)]
    if d == 0:
        edge = [jnp.where(sub == 0, halo[m], pltpu.roll(x[seg - 1 - m], 1, 0)) for m in range(3)]
        at = lambda j: x[j] if j >= 0 else edge[-j - 1]
        taps = lambda j: [at(j - k) for k in range(CONV_W)]
        new_halo = [x[seg - 1 - m][SUBLANES - 1:SUBLANES] for m in range(3)]
    else:
        edge = [jnp.where(sub == SUBLANES - 1, halo[m], pltpu.roll(x[m], SUBLANES - 1, 0)) for m in range(3)]
        at = lambda j: x[j] if j < seg else edge[j - seg]
        taps = lambda j: [at(j + k) for k in range(CONV_W)]
        new_halo = [x[m][0:1] for m in range(3)]
    xc = []
    for j in range(seg):
        t = taps(j)
        acc = cb[d] + cw[d][CONV_W - 1] * t[0]
        for k in range(1, CONV_W):
            acc = acc + cw[d][CONV_W - 1 - k] * t[k]
        xc.append(acc)
    xc_all = jnp.concatenate(xc, axis=0)
    z = jnp.dot(xc_all.astype(BF16), w[d], preferred_element_type=F32)
    r = jax.nn.sigmoid(z[:, :LRU_BLK] + ba[d])
    ig = jax.nn.sigmoid(z[:, LRU_BLK:] + bx[d])
    log_a = decay[d] * r
    a_all = jnp.exp(log_a)
    th = jnp.tanh(log_a)
    u_all = jnp.sqrt(-2.0 * th / (1.0 - th)) * (ig * xc_all)
    a = [a_all[j * SUBLANES:(j + 1) * SUBLANES] for j in range(seg)]
    u = [u_all[j * SUBLANES:(j + 1) * SUBLANES] for j in range(seg)]
    order = list(range(seg)) if d == 0 else list(range(seg - 1, -1, -1))
    hl, pr = [None] * seg, [None] * seg
    prev = None
    for j in order:
        if prev is None:
            hl[j], pr[j] = u[j], a[j]
        else:
            hl[j] = a[j] * hl[prev] + u[j]
            pr[j] = a[j] * pr[prev]
        prev = j
    ai, ui = _sublane_scan(pr[prev], hl[prev], d == 1)
    end = ui + ai * carry
    if d == 0:
        start = jnp.where(sub == 0, carry, pltpu.roll(end, 1, 0))
        new_carry = end[SUBLANES - 1:SUBLANES]
    else:
        start = jnp.where(sub == SUBLANES - 1, carry, pltpu.roll(end, SUBLANES - 1, 0))
        new_carry = end[0:1]
    h = [hl[j] + pr[j] * start for j in range(seg)]
    return h, tuple(new_halo) + (new_carry,)


def _lru_kernel(xr_ref, yr_ref, cw_ref, cb_ref, w_ref, ba_ref, bx_ref, lam_ref, o_ref, *, seq):
    chunk = SUBLANES * LRU_SEG
    nfull = seq // chunk
    tail_seg = (seq - nfull * chunk) // SUBLANES
    prm = (
        [[cw_ref[d, j:j + 1, :] for j in range(CONV_W)] for d in range(2)],
        [cb_ref[d] for d in range(2)],
        [w_ref[d, 0] for d in range(2)],
        [ba_ref[d] for d in range(2)],
        [bx_ref[d] for d in range(2)],
        [-LRU_C * jax.nn.softplus(-lam_ref[d]) for d in range(2)],
    )
    rows = lambda base, j, seg: (0, pl.ds(base + j, SUBLANES, stride=seg), slice(None))

    def forward(base, seg, state):
        h, state = _lru_chunk(xr_ref, base, seg, 0, prm, state)
        for j in range(seg):
            o_ref[rows(base, j, seg)] = h[j]
        return state

    def backward(base, seg, state):
        h, state = _lru_chunk(xr_ref, base, seg, 1, prm, state)
        for j in range(seg):
            o_ref[rows(base, j, seg)] = (o_ref[rows(base, j, seg)] + h[j]) * jax.nn.gelu(yr_ref[rows(base, j, seg)])
        return state

    zero = jnp.zeros((1, LRU_BLK), F32)
    state = lax.fori_loop(0, nfull, lambda k, st: forward(k * chunk, LRU_SEG, st), (zero,) * 4)
    if tail_seg:
        forward(nfull * chunk, tail_seg, state)
    state = (zero,) * 4
    if tail_seg:
        state = backward(nfull * chunk, tail_seg, state)
    lax.fori_loop(0, nfull, lambda k, st: backward((nfull - 1 - k) * chunk, LRU_SEG, st), state)


def _lru(xy, cw, cb, w, ba, bx, lam, nseq, seq):
    tail = seq % (SUBLANES * LRU_SEG)
    assert tail % SUBLANES == 0 and (tail == 0 or tail // SUBLANES >= CONV_W - 1)
    xy3 = xy.reshape(nseq, seq, 2 * LRU_W)
    vec = lambda: pl.BlockSpec((2, 1, LRU_BLK), lambda b, c: (0, 0, c))
    return pl.pallas_call(
        functools.partial(_lru_kernel, seq=seq),
        grid=(nseq, LRU_BLOCKS),
        in_specs=[
            pl.BlockSpec((1, seq, LRU_BLK), lambda b, c: (b, 0, c)),
            pl.BlockSpec((1, seq, LRU_BLK), lambda b, c: (b, 0, LRU_BLOCKS + c)),
            pl.BlockSpec((2, CONV_W, LRU_BLK), lambda b, c: (0, 0, c)),
            vec(),
            pl.BlockSpec((2, 1, LRU_BLK, 2 * LRU_BLK), lambda b, c: (0, c, 0, 0)),
            vec(), vec(), vec(),
        ],
        out_specs=pl.BlockSpec((1, seq, LRU_BLK), lambda b, c: (b, 0, c)),
        out_shape=jax.ShapeDtypeStruct((nseq, seq, LRU_W), F32),
        compiler_params=_params(("arbitrary", "arbitrary")),
    )(xy3, xy3, cw, cb.reshape(2, 1, LRU_W), w, ba.reshape(2, 1, LRU_W),
      bx.reshape(2, 1, LRU_W), lam.reshape(2, 1, LRU_W))


def _outproj_kernel(attn_ref, lru_ref, x_ref, ga_ref, gl_ref, w_ref, g2_ref, wr_ref, x1_ref, h2_ref, aff_ref):
    an = _rms(attn_ref[...], ga_ref[...]).astype(BF16)
    ln = _rms(lru_ref[...], gl_ref[...]).astype(BF16)
    y = jnp.dot(an, w_ref[:ATTN_W], preferred_element_type=F32)
    y = y + jnp.dot(ln, w_ref[ATTN_W:], preferred_element_type=F32)
    x1 = x_ref[...] + y
    x1_ref[...] = x1
    hf = _rms(x1, g2_ref[...])
    h_hi = hf.astype(BF16)
    h2_ref[...] = h_hi
    h_lo = (hf - h_hi.astype(F32)).astype(BF16)
    nt = (((1,), (1,)), ((), ()))
    a = lax.dot_general(wr_ref[...], h_hi, nt, preferred_element_type=F32)
    b = lax.dot_general(wr_ref[:N_EXPERTS], h_lo, nt, preferred_element_type=F32)
    logits = a[:N_EXPERTS] + a[N_EXPERTS:] + b
    e = jnp.exp(logits - jnp.max(logits, axis=0, keepdims=True))
    aff_ref[...] = e / jnp.sum(e, axis=0, keepdims=True)


def _outproj(attn, lru, x, ga, gl, w, g2, wr_t):
    n = x.shape[0]
    row = lambda w_: pl.BlockSpec((ROW_TILE, w_), lambda i: (i, 0))
    full = lambda a: pl.BlockSpec(a.shape, lambda i: (0,) * a.ndim)
    return pl.pallas_call(
        _outproj_kernel,
        grid=(n // ROW_TILE,),
        in_specs=[row(ATTN_W), row(LRU_W), row(D_MODEL), full(ga), full(gl),
                  pl.BlockSpec(w.shape, lambda i: (0, 0), pipeline_mode=pl.Buffered(1)),
                  full(g2), full(wr_t)],
        out_specs=[row(D_MODEL), row(D_MODEL), pl.BlockSpec((N_EXPERTS, ROW_TILE), lambda i: (0, i))],
        out_shape=[
            jax.ShapeDtypeStruct((n, D_MODEL), F32),
            jax.ShapeDtypeStruct((n, D_MODEL), BF16),
            jax.ShapeDtypeStruct((N_EXPERTS, n), F32),
        ],
        compiler_params=_params(("arbitrary",)),
    )(attn, lru, x, ga, gl, w, g2, wr_t)


def _split_router(w_router):
    wt = w_router.astype(F32).T
    hi = wt.astype(BF16)
    lo = (wt - hi.astype(F32)).astype(BF16)
    return jnp.concatenate([hi, lo], axis=0)


def _threshold_kernel(aff_ref, thr_ref, need_ref, gt_ref, eq_ref, *, groups):
    for gi, (lo, hi, cap) in enumerate(groups):
        def keys():
            return lax.bitcast_convert_type(aff_ref[:, lo:hi], I32)

        def step(it, thr):
            cand = thr | jnp.left_shift(jnp.int32(1), 30 - it)
            cnt = jnp.sum((keys() >= cand).astype(F32), axis=1, keepdims=True)
            return jnp.where(cnt >= cap, cand, thr)

        thr = lax.fori_loop(0, 31, step, jnp.zeros((N_EXPERTS, 1), I32))
        ngt = jnp.sum((keys() > thr).astype(F32), axis=1, keepdims=True)
        thr_ref[gi] = jnp.broadcast_to(thr, (N_EXPERTS, LANES))
        need_ref[gi] = jnp.broadcast_to(cap - ngt, (N_EXPERTS, LANES))
        for t in range(lo // ROUTE_TILE, hi // ROUTE_TILE):
            kt = lax.bitcast_convert_type(aff_ref[:, t * ROUTE_TILE:(t + 1) * ROUTE_TILE], I32)
            gt_ref[t] = jnp.broadcast_to(jnp.sum((kt > thr).astype(F32), axis=1, keepdims=True), (N_EXPERTS, LANES))
            eq_ref[t] = jnp.broadcast_to(jnp.sum((kt == thr).astype(F32), axis=1, keepdims=True), (N_EXPERTS, LANES))


def _thresholds(aff_t, groups):
    n = aff_t.shape[1]
    nt = n // ROUTE_TILE
    ng = len(groups)
    return pl.pallas_call(
        functools.partial(_threshold_kernel, groups=groups),
        out_shape=[
            jax.ShapeDtypeStruct((ng, N_EXPERTS, LANES), I32),
            jax.ShapeDtypeStruct((ng, N_EXPERTS, LANES), F32),
            jax.ShapeDtypeStruct((nt, N_EXPERTS, LANES), F32),
            jax.ShapeDtypeStruct((nt, N_EXPERTS, LANES), F32),
        ],
        compiler_params=pltpu.CompilerParams(vmem_limit_bytes=VMEM_LIMIT),
    )(aff_t)


def _segment_pieces(tile, cpad_s, off_s, fn):
    for e in range(N_EXPERTS):
        npiece = cpad_s[tile * N_EXPERTS + e] // PIECE
        first = off_s[tile * N_EXPERTS + e]

        def body(j, _, e=e, first=first):
            fn(e, j, pl.multiple_of(first + j * PIECE, PIECE))
            return 0

        lax.fori_loop(0, npiece, body, 0)


def _piece_onehot(gpos, e, row):
    rid = lax.broadcasted_iota(I32, (PIECE, gpos.shape[1]), 0) + row
    return gpos[e:e + 1, :] == rid


def _dispatch_kernel(base_s, cpad_s, off_s, nch_s, tail_s,
                     h2_ref, aff_ref, thr_ref, need_ref, ceq_ref, offv_ref,
                     xe_ref, gpos_ref, stage, pbuf, zeros, sem, *, tile_group, cap_rows):
    i = pl.program_id(0)
    tm = ROUTE_TILE

    @pl.when(i == 0)
    def _():
        pbuf[...] = jnp.zeros(pbuf.shape, BF16)

    grp = tile_group(i)
    thr = jnp.where(grp == 0, thr_ref[0], thr_ref[1])[:, :1]
    need = jnp.where(grp == 0, need_ref[0], need_ref[1])[:, :1]
    aff = aff_ref[...]
    keys = lax.bitcast_convert_type(aff, I32)
    gt = keys > thr
    eq = keys == thr
    before = (lax.broadcasted_iota(I32, (tm, tm), 0) < lax.broadcasted_iota(I32, (tm, tm), 1))
    before = jnp.where(before, 1.0, 0.0).astype(BF16)
    eq_rank = jnp.dot(jnp.where(eq, 1.0, 0.0).astype(BF16), before, preferred_element_type=F32)
    sel = gt | (eq & (ceq_ref[0][:, :1] + eq_rank < need))
    rank = jnp.dot(jnp.where(sel, 1.0, 0.0).astype(BF16), before, preferred_element_type=F32)
    gpos = jnp.where(sel, offv_ref[0][:, :1] + rank.astype(I32), -1)
    gpos_ref[...] = gpos

    g_hi = aff.astype(BF16).astype(F32)
    g_mid = (aff - g_hi).astype(BF16).astype(F32)
    g_lo = (aff - g_hi - g_mid).astype(BF16).astype(F32)
    pad = jnp.zeros((GATE_COLS - 3 * N_EXPERTS, tm), F32)
    gate_cols = jnp.concatenate([g_hi, g_mid, g_lo, pad], axis=0).T.astype(BF16)

    def build(e, j, row):
        pbuf[pl.ds(row, PIECE), :] = jnp.where(_piece_onehot(gpos, e, row), 1.0, 0.0).astype(BF16)

    _segment_pieces(i, cpad_s, off_s, build)

    def chunk(c, _):
        r0 = pl.multiple_of(c * MXU_DIM, MXU_DIM)
        onehot = pbuf[pl.ds(r0, MXU_DIM), :]
        rows = jnp.dot(onehot, h2_ref[...], preferred_element_type=F32)
        stage[pl.ds(r0, MXU_DIM), pl.ds(0, D_MODEL)] = rows.astype(BF16)
        gates = jnp.dot(onehot, gate_cols, preferred_element_type=F32)
        stage[pl.ds(r0, MXU_DIM), pl.ds(D_MODEL, GATE_COLS)] = gates.astype(BF16)
        return 0

    lax.fori_loop(0, nch_s[i], chunk, 0)

    def copy(e, j, row):
        dst = pl.multiple_of(e * cap_rows + base_s[i * N_EXPERTS + e] + j * PIECE, PIECE)
        return pltpu.make_async_copy(stage.at[pl.ds(row, PIECE)], xe_ref.at[pl.ds(dst, PIECE)], sem)

    _segment_pieces(i, cpad_s, off_s, lambda e, j, row: copy(e, j, row).start())
    _segment_pieces(i, cpad_s, off_s, lambda e, j, row: copy(e, j, row).wait())

    @pl.when(i == pl.num_programs(0) - 1)
    def _():
        zeros[...] = jnp.zeros(zeros.shape, BF16)
        for e in range(N_EXPERTS):
            def mk(j, e=e):
                dst = pl.multiple_of(e * cap_rows + tail_s[e] + j * PIECE, PIECE)
                return pltpu.make_async_copy(zeros, xe_ref.at[pl.ds(dst, PIECE)], sem)

            def go(j, _, mk=mk):
                mk(j).start()
                return 0

            def done(j, _, mk=mk):
                mk(j).wait()
                return 0

            ntail = tail_s[N_EXPERTS + e]
            lax.fori_loop(0, ntail, go, 0)
            lax.fori_loop(0, ntail, done, 0)


def _dispatch(meta, h2, aff_t, thr, need, tile_group, cap_rows):
    n = h2.shape[0]
    nt = n // ROUTE_TILE
    tile3 = lambda: pl.BlockSpec((1, N_EXPERTS, LANES), lambda i, *_: (i, 0, 0))
    full3 = lambda a: pl.BlockSpec(a.shape, lambda i, *_: (0, 0, 0))
    grid_spec = pltpu.PrefetchScalarGridSpec(
        num_scalar_prefetch=5,
        grid=(nt,),
        in_specs=[
            pl.BlockSpec((ROUTE_TILE, D_MODEL), lambda i, *_: (i, 0)),
            pl.BlockSpec((N_EXPERTS, ROUTE_TILE), lambda i, *_: (0, i)),
            full3(thr), full3(need), tile3(), tile3(),
        ],
        out_specs=[
            pl.BlockSpec(memory_space=pl.ANY),
            pl.BlockSpec((N_EXPERTS, ROUTE_TILE), lambda i, *_: (0, i)),
        ],
        scratch_shapes=[
            pltpu.VMEM((N_EXPERTS * ROUTE_TILE, XE_W), BF16),
            pltpu.VMEM((N_EXPERTS * ROUTE_TILE, ROUTE_TILE), BF16),
            pltpu.VMEM((PIECE, XE_W), BF16),
            pltpu.SemaphoreType.DMA,
        ],
    )
    return pl.pallas_call(
        functools.partial(_dispatch_kernel, tile_group=tile_group, cap_rows=cap_rows),
        grid_spec=grid_spec,
        out_shape=[
            jax.ShapeDtypeStruct((N_EXPERTS * cap_rows, XE_W), BF16),
            jax.ShapeDtypeStruct((N_EXPERTS, n), I32),
        ],
        compiler_params=_params(("arbitrary",)),
    )(meta["base"], meta["cpad"], meta["off"], meta["nch"], meta["tail"],
      h2, aff_t, thr, need, meta["ceq_v"], meta["off_v"])


def _expert_kernel(used_s, xe_ref, wg_ref, wu_ref, wd_ref, ye_ref):
    e = pl.program_id(0)
    r = pl.program_id(1)

    @pl.when(r * EXPERT_TILE < used_s[e])
    def _():
        x = xe_ref[:, :D_MODEL]
        gc = xe_ref[:, D_MODEL:].astype(F32)
        lane = lax.broadcasted_iota(I32, gc.shape, 1)
        gate = jnp.sum(jnp.where((lane & (N_EXPERTS - 1)) == e, gc, 0.0), axis=1, keepdims=True)
        hg = jnp.dot(x, wg_ref[0], preferred_element_type=F32)
        hu = jnp.dot(x, wu_ref[0], preferred_element_type=F32)
        hid = (jax.nn.silu(hg) * hu).astype(BF16)
        y = jnp.dot(hid, wd_ref[0], preferred_element_type=F32) * gate
        ye_ref[...] = y.astype(BF16)


def _experts(used, xe, wg, wu, wd, cap_rows):
    rt = cap_rows // EXPERT_TILE

    def rows(e, r, used_s):
        last = jnp.maximum((used_s[e] + EXPERT_TILE - 1) // EXPERT_TILE - 1, 0)
        return (e * rt + jnp.minimum(r, last), 0)

    grid_spec = pltpu.PrefetchScalarGridSpec(
        num_scalar_prefetch=1,
        grid=(N_EXPERTS, rt),
        in_specs=[
            pl.BlockSpec((EXPERT_TILE, XE_W), rows),
            pl.BlockSpec((1, D_MODEL, D_EXPERT), lambda e, r, u: (e, 0, 0)),
            pl.BlockSpec((1, D_MODEL, D_EXPERT), lambda e, r, u: (e, 0, 0)),
            pl.BlockSpec((1, D_EXPERT, D_MODEL), lambda e, r, u: (e, 0, 0)),
        ],
        out_specs=pl.BlockSpec((EXPERT_TILE, D_MODEL), rows),
    )
    return pl.pallas_call(
        _expert_kernel,
        grid_spec=grid_spec,
        out_shape=jax.ShapeDtypeStruct((N_EXPERTS * cap_rows, D_MODEL), BF16),
        compiler_params=_params(("arbitrary", "arbitrary")),
    )(used, xe, wg, wu, wd)


def _combine_kernel(base_s, cpad_s, off_s, nch_s, rall_s, x1_ref, gpos_ref, ye_ref, x2_ref,
                    stage, pbuf, sem, *, cap_rows):
    i = pl.program_id(0)

    @pl.when(i == 0)
    def _():
        stage[...] = jnp.zeros(stage.shape, BF16)

    gpos = gpos_ref[...]

    def copy(e, j, row):
        src = pl.multiple_of(e * cap_rows + base_s[i * N_EXPERTS + e] + j * PIECE, PIECE)
        return pltpu.make_async_copy(ye_ref.at[pl.ds(src, PIECE)], stage.at[pl.ds(row, PIECE)], sem)

    def fetch(e, j, row):
        copy(e, j, row).start()
        pbuf[pl.ds(row, PIECE), :] = jnp.where(_piece_onehot(gpos, e, row), 1.0, 0.0).astype(BF16)

    _segment_pieces(i, cpad_s, off_s, fetch)

    def clear(j, _):
        row = pl.multiple_of(rall_s[i] + j * PIECE, PIECE)
        pbuf[pl.ds(row, PIECE), :] = jnp.zeros((PIECE, ROUTE_TILE), BF16)
        return 0

    lax.fori_loop(0, (nch_s[i] * MXU_DIM - rall_s[i]) // PIECE, clear, 0)
    x2_ref[...] = x1_ref[...]
    _segment_pieces(i, cpad_s, off_s, lambda e, j, row: copy(e, j, row).wait())

    def chunk(c, _):
        r0 = pl.multiple_of(c * MXU_DIM, MXU_DIM)
        onehot_t = pbuf[pl.ds(r0, MXU_DIM), :].astype(F32).T.astype(BF16)
        x2_ref[...] += jnp.dot(onehot_t, stage[pl.ds(r0, MXU_DIM), :], preferred_element_type=F32)
        return 0

    lax.fori_loop(0, nch_s[i], chunk, 0)


def _combine(meta, x1, gpos, ye, cap_rows):
    n = x1.shape[0]
    grid_spec = pltpu.PrefetchScalarGridSpec(
        num_scalar_prefetch=5,
        grid=(n // ROUTE_TILE,),
        in_specs=[
            pl.BlockSpec((ROUTE_TILE, D_MODEL), lambda i, *_: (i, 0)),
            pl.BlockSpec((N_EXPERTS, ROUTE_TILE), lambda i, *_: (0, i)),
            pl.BlockSpec(memory_space=pl.ANY),
        ],
        out_specs=pl.BlockSpec((ROUTE_TILE, D_MODEL), lambda i, *_: (i, 0)),
        scratch_shapes=[
            pltpu.VMEM((N_EXPERTS * ROUTE_TILE, D_MODEL), BF16),
            pltpu.VMEM((N_EXPERTS * ROUTE_TILE, ROUTE_TILE), BF16),
            pltpu.SemaphoreType.DMA,
        ],
    )
    return pl.pallas_call(
        functools.partial(_combine_kernel, cap_rows=cap_rows),
        grid_spec=grid_spec,
        out_shape=jax.ShapeDtypeStruct((n, D_MODEL), F32),
        compiler_params=_params(("arbitrary",)),
    )(meta["base"], meta["cpad"], meta["off"], meta["nch"], meta["rall"], x1, gpos, ye)


def _final_kernel(x_ref, g_ref, op_ref, os_ref, *, tiles_p):
    i = pl.program_id(0)
    y = _rms(x_ref[...], g_ref[...])

    @pl.when(i < tiles_p)
    def _():
        op_ref[...] = y

    @pl.when(i >= tiles_p)
    def _():
        os_ref[...] = y


def _final_norm(x, g, n_p):
    n = x.shape[0]
    tiles_p = n_p // ROW_TILE
    return pl.pallas_call(
        functools.partial(_final_kernel, tiles_p=tiles_p),
        grid=(n // ROW_TILE,),
        in_specs=[pl.BlockSpec((ROW_TILE, D_MODEL), lambda i: (i, 0)),
                  pl.BlockSpec((1, D_MODEL), lambda i: (0, 0))],
        out_specs=[
            pl.BlockSpec((ROW_TILE, D_MODEL), lambda i: (jnp.minimum(i, tiles_p - 1), 0)),
            pl.BlockSpec((ROW_TILE, D_MODEL), lambda i: (jnp.maximum(i - tiles_p, 0), 0)),
        ],
        out_shape=[
            jax.ShapeDtypeStruct((n_p, D_MODEL), F32),
            jax.ShapeDtypeStruct((n - n_p, D_MODEL), F32),
        ],
        compiler_params=_params(("arbitrary",)),
    )(x, g)


def _route_meta(gtc, eqc, need, tile_grp, group_first_tile, cap_rows):
    gtc = gtc[:, :, 0].astype(I32)
    eqc = eqc[:, :, 0].astype(I32)
    need_t = need[:, :, 0].astype(I32)[tile_grp]
    ceq = jnp.cumsum(eqc, axis=0) - eqc
    ceq = ceq - ceq[group_first_tile]
    cnt = gtc + jnp.clip(need_t - ceq, 0, eqc)
    cpad = (cnt + PIECE - 1) // PIECE * PIECE
    base = jnp.cumsum(cpad, axis=0) - cpad
    off = jnp.cumsum(cpad, axis=1) - cpad
    used = jnp.sum(cpad, axis=0)
    rall = jnp.sum(cpad, axis=1)
    nch = (rall + MXU_DIM - 1) // MXU_DIM
    tail_n = ((used + EXPERT_TILE - 1) // EXPERT_TILE * EXPERT_TILE - used) // PIECE
    bcast = lambda a, dt: jnp.broadcast_to(a[:, :, None].astype(dt), a.shape + (LANES,))
    return {
        "base": base.reshape(-1), "cpad": cpad.reshape(-1), "off": off.reshape(-1), "nch": nch, "rall": rall,
        "used": used, "tail": jnp.concatenate([used, tail_n]),
        "ceq_v": bcast(ceq, F32), "off_v": bcast(off, I32),
    }


def kernel(x_prompt, x_sample, rel_bias, norm1_g, w_in, attn_sink, conv_w, conv_b, w_rec_a, b_rec_a,
           w_rec_x, b_rec_x, lru_lambda, grp_g_attn, grp_g_lru, w_out, norm2_g, w_router,
           w_e_gate, w_e_up, w_e_down, final_g):
    bp, seq, _ = x_prompt.shape
    bs = x_sample.shape[0]
    assert x_sample.shape[1] == seq
    nseq = bp + bs
    n_p, n_s = bp * seq, bs * seq
    n = n_p + n_s
    assert seq % ATTN_TQ == 0 and n_p % ROW_TILE == 0 and n_s % ROW_TILE == 0
    assert n_p % ROUTE_TILE == 0 and n_s % ROUTE_TILE == 0
    depth = w_in.shape[0]

    groups = ((0, n_p, EC_CAPACITY * n_p // N_EXPERTS), (n_p, n, EC_CAPACITY * n_s // N_EXPERTS))
    nt = n // ROUTE_TILE
    nt_p = n_p // ROUTE_TILE
    tile_grp = np.where(np.arange(nt) < nt_p, 0, 1)
    group_first_tile = np.where(np.arange(nt) < nt_p, 0, nt_p)
    real_rows = groups[0][2] + groups[1][2]
    cap_rows = -(-(real_rows + (PIECE - 1) * nt) // EXPERT_TILE) * EXPERT_TILE
    tile_group = lambda i: (i >= nt_p).astype(I32)

    bias_tbl = _bias_table(rel_bias)
    row = lambda v: v.reshape(1, -1).astype(F32)
    x = None

    for l in range(depth):
        w_in_l = w_in[l].astype(BF16)
        if l == 0:
            qkv, xy, x = _inproj_first(x_prompt.reshape(n_p, D_MODEL), x_sample.reshape(n_s, D_MODEL),
                                       row(norm1_g[l]), w_in_l)
        else:
            qkv, xy = _inproj(x, row(norm1_g[l]), w_in_l)
        sink_tbl = jnp.broadcast_to(attn_sink[l].astype(F32)[:, None, None], (N_HEADS, 1, LANES))
        attn = _attention(qkv, bias_tbl, sink_tbl, nseq, seq).reshape(n, ATTN_W)
        w_rec = jnp.concatenate([w_rec_a[l], w_rec_x[l]], axis=-1).astype(BF16)
        lru = _lru(xy, conv_w[l].astype(F32), conv_b[l].astype(F32), w_rec, b_rec_a[l].astype(F32),
                   b_rec_x[l].astype(F32), lru_lambda[l].astype(F32), nseq, seq).reshape(n, LRU_W)
        x1, h2, aff_t = _outproj(attn, lru, x, row(grp_g_attn[l]), row(grp_g_lru[l]),
                                 w_out[l].astype(BF16), row(norm2_g[l]), _split_router(w_router[l]))
        thr, need, gtc, eqc = _thresholds(aff_t, groups)
        meta = _route_meta(gtc, eqc, need, tile_grp, group_first_tile, cap_rows)
        xe, gpos = _dispatch(meta, h2, aff_t, thr, need, tile_group, cap_rows)
        ye = _experts(meta["used"], xe, w_e_gate[l].astype(BF16), w_e_up[l].astype(BF16),
                      w_e_down[l].astype(BF16), cap_rows)
        x = _combine(meta, x1, gpos, ye, cap_rows)

    y_p, y_s = _final_norm(x, row(final_g), n_p)
    return (y_p.reshape(bp, seq, D_MODEL), y_s.reshape(bs, seq, D_MODEL))
```

```python
import functools

import numpy as np
import jax
import jax.numpy as jnp
from jax import lax
from jax.experimental import pallas as pl
from jax.experimental.pallas import tpu as pltpu

F32 = jnp.float32
BF16 = jnp.bfloat16
I32 = jnp.int32

D_MODEL = 2048
HEAD_DIM = 128
N_HEADS = 8
N_KV_HEADS = 2
GQA_GROUP = N_HEADS // N_KV_HEADS
ATTN_W = N_HEADS * HEAD_DIM
KV_W = N_KV_HEADS * HEAD_DIM
QKV_W = ATTN_W + 2 * KV_W
WINDOW = 128
BLOCK = 128
N_BUCKETS = 32
MAX_DISTANCE = 128
LRU_W = D_MODEL - ATTN_W
LRU_BLOCKS = 8
LRU_BLK = LRU_W // LRU_BLOCKS
CONV_W = 4
LRU_C = 8.0
N_EXPERTS = 16
EC_CAPACITY = 2
D_EXPERT = D_MODEL // 2
EPS = 1e-6
NEG = -1e30

LANES = 128
SUBLANES = 8
BF16_ROWS = 16
MXU_DIM = 256
VMEM_LIMIT = 56 * 1024 * 1024

ROW_TILE = 512
ATTN_TQ = 512
ATTN_SKEW = 2
LRU_SEG = 60
ROUTE_TILE = 256
PIECE = BF16_ROWS
EXPERT_TILE = 512
GATE_COLS = LANES
XE_W = D_MODEL + GATE_COLS


def _params(sem, **kw):
    return pltpu.CompilerParams(dimension_semantics=sem, vmem_limit_bytes=VMEM_LIMIT, **kw)


def _rms(x, g):
    return x * lax.rsqrt(jnp.mean(x * x, axis=-1, keepdims=True) + EPS) * g


def _inproj_kernel(x_ref, g_ref, w_ref, qkv_ref, xy_ref):
    h = _rms(x_ref[...], g_ref[...]).astype(BF16)
    qkv_ref[...] = jnp.dot(h, w_ref[:, :QKV_W], preferred_element_type=F32).astype(BF16)
    xy_ref[...] = jnp.dot(h, w_ref[:, QKV_W:], preferred_element_type=F32)


def _layer_weight(w, l):
    return pl.BlockSpec((None,) + w.shape[1:], lambda i: (l, 0, 0), pipeline_mode=pl.Buffered(1))


def _inproj(x, g, w, l):
    n = x.shape[0]
    return pl.pallas_call(
        _inproj_kernel,
        grid=(n // ROW_TILE,),
        in_specs=[
            pl.BlockSpec((ROW_TILE, D_MODEL), lambda i: (i, 0)),
            pl.BlockSpec((1, D_MODEL), lambda i: (0, 0)),
            _layer_weight(w, l),
        ],
        out_specs=[
            pl.BlockSpec((ROW_TILE, QKV_W), lambda i: (i, 0)),
            pl.BlockSpec((ROW_TILE, 2 * LRU_W), lambda i: (i, 0)),
        ],
        out_shape=[
            jax.ShapeDtypeStruct((n, QKV_W), BF16),
            jax.ShapeDtypeStruct((n, 2 * LRU_W), F32),
        ],
        compiler_params=_params(("arbitrary",)),
    )(x, g, w)


def _inproj_first_kernel(xp_ref, xs_ref, g_ref, w_ref, qkv_ref, xy_ref, x_ref, *, tiles_p):
    i = pl.program_id(0)

    def run(src):
        x_ref[...] = src[...]
        _inproj_kernel(src, g_ref, w_ref, qkv_ref, xy_ref)

    pl.when(i < tiles_p)(lambda: run(xp_ref))
    pl.when(i >= tiles_p)(lambda: run(xs_ref))


def _inproj_first(xp, xs, g, w):
    n_p, n_s = xp.shape[0], xs.shape[0]
    n = n_p + n_s
    tile = ROW_TILE // 2
    tiles_p = n_p // tile
    row = lambda w_: pl.BlockSpec((tile, w_), lambda i: (i, 0))
    return pl.pallas_call(
        functools.partial(_inproj_first_kernel, tiles_p=tiles_p),
        grid=(n // tile,),
        in_specs=[
            pl.BlockSpec((tile, D_MODEL), lambda i: (jnp.minimum(i, tiles_p - 1), 0)),
            pl.BlockSpec((tile, D_MODEL), lambda i: (jnp.maximum(i - tiles_p, 0), 0)),
            pl.BlockSpec((1, D_MODEL), lambda i: (0, 0)),
            _layer_weight(w, 0),
        ],
        out_specs=[row(QKV_W), row(2 * LRU_W), row(D_MODEL)],
        out_shape=[
            jax.ShapeDtypeStruct((n, QKV_W), BF16),
            jax.ShapeDtypeStruct((n, 2 * LRU_W), F32),
            jax.ShapeDtypeStruct((n, D_MODEL), F32),
        ],
        compiler_params=_params(("arbitrary",)),
    )(xp, xs, g, w)


def _attn_kernel(q_ref, kp_ref, km_ref, kn_ref, vp_ref, vm_ref, vn_ref, bias_ref, sink_ref, o_ref,
                 s_scr, p_scr, m_scr, d_scr, *, seq):
    i = pl.program_id(2)
    kwin = jnp.concatenate([kp_ref[0], km_ref[0], kn_ref[0]], axis=0)
    vwin = jnp.concatenate([vp_ref[0], vm_ref[0], vn_ref[0]], axis=0)
    qi = lax.broadcasted_iota(I32, (BLOCK, 3 * BLOCK), 0)
    ji = lax.broadcasted_iota(I32, (BLOCK, 3 * BLOCK), 1)
    band = jnp.abs(ji - BLOCK - qi) <= WINDOW
    scale = HEAD_DIM ** -0.5
    nsb = ATTN_TQ // BLOCK
    unit = lambda sb, g: sb * GQA_GROUP + g
    qrows = lambda sb: slice(sb * BLOCK, (sb + 1) * BLOCK)
    hcols = lambda g: slice(g * HEAD_DIM, (g + 1) * HEAD_DIM)
    sink = [sink_ref[g][:, :1] for g in range(GQA_GROUP)]
    ones = jnp.ones((3 * BLOCK, HEAD_DIM), BF16)
    valid, vext = [], []
    for sb in range(nsb):
        kpos = i * ATTN_TQ + (sb - 1) * BLOCK + ji
        valid.append(band & (kpos >= 0) & (kpos < seq))
        vext.append(jnp.concatenate([vwin[sb * BLOCK:sb * BLOCK + 3 * BLOCK], ones], axis=1))

    def scores(sb, g):
        s = lax.dot_general(q_ref[0, qrows(sb), hcols(g)], kwin[sb * BLOCK:sb * BLOCK + 3 * BLOCK],
                            (((1,), (1,)), ((), ())), preferred_element_type=F32)
        s_scr[unit(sb, g)] = jnp.where(valid[sb], s * scale + bias_ref[g], NEG)

    def row_max(sb, g):
        u = unit(sb, g)
        m_scr[u] = jnp.maximum(jnp.max(s_scr[u], axis=-1, keepdims=True), sink[g])

    def exponent(sb, g):
        u = unit(sb, g)
        p_scr[u] = jnp.exp(s_scr[u] - m_scr[u]).astype(BF16)

    def values(sb, g):
        u = unit(sb, g)
        o = jnp.dot(p_scr[u], vext[sb], preferred_element_type=F32)
        denom = o[:, HEAD_DIM:HEAD_DIM + 1] + jnp.exp(sink[g] - m_scr[u])
        o_ref[0, qrows(sb), hcols(g)] = o[:, :HEAD_DIM] / denom

    passes = (scores, row_max, exponent, values)
    units = [(sb, g) for sb in range(nsb) for g in range(GQA_GROUP)]
    for t in range(len(units) + ATTN_SKEW * (len(passes) - 1)):
        for k, fn in enumerate(passes):
            if 0 <= t - ATTN_SKEW * k < len(units):
                fn(*units[t - ATTN_SKEW * k])


def _attention(qkv, bias_tbl, sink_tbl, nseq, seq):
    qkv3 = qkv.reshape(nseq, seq, QKV_W)
    per = ATTN_TQ // BLOCK
    last = seq // BLOCK - 1
    kcol = ATTN_W // HEAD_DIM
    vcol = (ATTN_W + KV_W) // HEAD_DIM

    def side(col, prev):
        if prev:
            return pl.BlockSpec((1, BLOCK, HEAD_DIM), lambda b, k, i: (b, jnp.maximum(i * per - 1, 0), col + k))
        return pl.BlockSpec((1, BLOCK, HEAD_DIM), lambda b, k, i: (b, jnp.minimum(i * per + per, last), col + k))

    def main(col):
        return pl.BlockSpec((1, ATTN_TQ, HEAD_DIM), lambda b, k, i: (b, i, col + k))

    qw = GQA_GROUP * HEAD_DIM
    return pl.pallas_call(
        functools.partial(_attn_kernel, seq=seq),
        grid=(nseq, N_KV_HEADS, seq // ATTN_TQ),
        in_specs=[
            pl.BlockSpec((1, ATTN_TQ, qw), lambda b, k, i: (b, i, k)),
            side(kcol, True), main(kcol), side(kcol, False),
            side(vcol, True), main(vcol), side(vcol, False),
            pl.BlockSpec((GQA_GROUP, BLOCK, 3 * BLOCK), lambda b, k, i: (k, 0, 0)),
            pl.BlockSpec((GQA_GROUP, 1, LANES), lambda b, k, i: (k, 0, 0)),
        ],
        out_specs=pl.BlockSpec((1, ATTN_TQ, qw), lambda b, k, i: (b, i, k)),
        out_shape=jax.ShapeDtypeStruct((nseq, seq, ATTN_W), F32),
        scratch_shapes=[
            pltpu.VMEM((per * GQA_GROUP, BLOCK, 3 * BLOCK), F32),
            pltpu.VMEM((per * GQA_GROUP, BLOCK, 3 * BLOCK), BF16),
            pltpu.VMEM((per * GQA_GROUP, BLOCK, 1), F32),
            pltpu.VMEM((per * GQA_GROUP, BLOCK, 1), F32),
        ],
        compiler_params=_params(("arbitrary", "arbitrary", "arbitrary")),
    )(qkv3, qkv3, qkv3, qkv3, qkv3, qkv3, qkv3, bias_tbl, sink_tbl)


def _t5_bucket(rel):
    nb = N_BUCKETS // 2
    ret = (rel > 0).astype(np.int32) * nb
    n = np.abs(rel)
    max_exact = nb // 2
    large = max_exact + (np.log(np.maximum(n, 1) / max_exact) / np.log(MAX_DISTANCE / max_exact)
                         * (nb - max_exact)).astype(np.int32)
    large = np.minimum(large, nb - 1)
    return (ret + np.where(n < max_exact, n, large)).astype(np.int32)


def _bias_table(rel_bias):
    rel = (np.arange(3 * BLOCK)[None, :] - BLOCK) - np.arange(BLOCK)[:, None]
    onehot = np.eye(N_BUCKETS, dtype=np.float32)[_t5_bucket(rel)]
    return jnp.einsum("qkb,bh->hqk", onehot, rel_bias.astype(F32), precision=lax.Precision.HIGHEST)


def _sublane_scan(a, u, reverse):
    sub = lax.broadcasted_iota(I32, a.shape, 0)
    for d in (1, 2, 4):
        keep = (sub < SUBLANES - d) if reverse else (sub >= d)
        shift = SUBLANES - d if reverse else d
        a_sh = jnp.where(keep, pltpu.roll(a, shift, 0), 1.0)
        u_sh = jnp.where(keep, pltpu.roll(u, shift, 0), 0.0)
        u = a * u_sh + u
        a = a * a_sh
    return a, u


def _lru_chunk(xr_ref, base, seg, d, prm, state):
    cw, cb, w, ba, bx, decay = prm
    halo, carry = state[:3], state[3]
    sub = lax.broadcasted_iota(I32, (SUBLANES, LRU_BLK), 0)
    x = [xr_ref[0, pl.ds(base + j, SUBLANES, stride=seg), :] for j in range(seg)]
    if d == 0:
        edge = [jnp.where(sub == 0, halo[m], pltpu.roll(x[seg - 1 - m], 1, 0)) for m in range(3)]
        at = lambda j: x[j] if j >= 0 else edge[-j - 1]
        taps = lambda j: [at(j - k) for k in range(CONV_W)]
        new_halo = [x[seg - 1 - m][SUBLANES - 1:SUBLANES] for m in range(3)]
    else:
        edge = [jnp.where(sub == SUBLANES - 1, halo[m], pltpu.roll(x[m], SUBLANES - 1, 0)) for m in range(3)]
        at = lambda j: x[j] if j < seg else edge[j - seg]
        taps = lambda j: [at(j + k) for k in range(CONV_W)]
        new_halo = [x[m][0:1] for m in range(3)]
    xc = []
    for j in range(seg):
        t = taps(j)
        acc = cb[d] + cw[d][CONV_W - 1] * t[0]
        for k in range(1, CONV_W):
            acc = acc + cw[d][CONV_W - 1 - k] * t[k]
        xc.append(acc)
    xc_all = jnp.concatenate(xc, axis=0)
    z = jnp.dot(xc_all.astype(BF16), w[d], preferred_element_type=F32)
    r = jax.nn.sigmoid(z[:, :LRU_BLK] + ba[d])
    ig = jax.nn.sigmoid(z[:, LRU_BLK:] + bx[d])
    log_a = decay[d] * r
    a_all = jnp.exp(log_a)
    th = jnp.tanh(log_a)
    u_all = jnp.sqrt(-2.0 * th / (1.0 - th)) * (ig * xc_all)
    a = [a_all[j * SUBLANES:(j + 1) * SUBLANES] for j in range(seg)]
    u = [u_all[j * SUBLANES:(j + 1) * SUBLANES] for j in range(seg)]
    order = list(range(seg)) if d == 0 else list(range(seg - 1, -1, -1))
    hl, pr = [None] * seg, [None] * seg
    prev = None
    for j in order:
        if prev is None:
            hl[j], pr[j] = u[j], a[j]
        else:
            hl[j] = a[j] * hl[prev] + u[j]
            pr[j] = a[j] * pr[prev]
        prev = j
    ai, ui = _sublane_scan(pr[prev], hl[prev], d == 1)
    end = ui + ai * carry
    if d == 0:
        start = jnp.where(sub == 0, carry, pltpu.roll(end, 1, 0))
        new_carry = end[SUBLANES - 1:SUBLANES]
    else:
        start = jnp.where(sub == SUBLANES - 1, carry, pltpu.roll(end, SUBLANES - 1, 0))
        new_carry = end[0:1]
    h = [hl[j] + pr[j] * start for j in range(seg)]
    return h, tuple(new_halo) + (new_carry,)


def _lru_kernel(xr_ref, yr_ref, cw_ref, cb_ref, w_ref, ba_ref, bx_ref, lam_ref, o_ref, *, seq):
    chunk = SUBLANES * LRU_SEG
    nfull = seq // chunk
    tail_seg = (seq - nfull * chunk) // SUBLANES
    prm = (
        [[cw_ref[d, j:j + 1, :] for j in range(CONV_W)] for d in range(2)],
        [cb_ref[d] for d in range(2)],
        [w_ref[d, 0] for d in range(2)],
        [ba_ref[d] for d in range(2)],
        [bx_ref[d] for d in range(2)],
        [-LRU_C * jax.nn.softplus(-lam_ref[d]) for d in range(2)],
    )
    rows = lambda base, j, seg: (0, pl.ds(base + j, SUBLANES, stride=seg), slice(None))

    def forward(base, seg, state):
        h, state = _lru_chunk(xr_ref, base, seg, 0, prm, state)
        for j in range(seg):
            o_ref[rows(base, j, seg)] = h[j]
        return state

    def backward(base, seg, state):
        h, state = _lru_chunk(xr_ref, base, seg, 1, prm, state)
        for j in range(seg):
            o_ref[rows(base, j, seg)] = (o_ref[rows(base, j, seg)] + h[j]) * jax.nn.gelu(yr_ref[rows(base, j, seg)])
        return state

    zero = jnp.zeros((1, LRU_BLK), F32)
    state = lax.fori_loop(0, nfull, lambda k, st: forward(k * chunk, LRU_SEG, st), (zero,) * 4)
    if tail_seg:
        forward(nfull * chunk, tail_seg, state)
    state = (zero,) * 4
    if tail_seg:
        state = backward(nfull * chunk, tail_seg, state)
    lax.fori_loop(0, nfull, lambda k, st: backward((nfull - 1 - k) * chunk, LRU_SEG, st), state)


def _lru(xy, cw, cb, w, ba, bx, lam, nseq, seq):
    tail = seq % (SUBLANES * LRU_SEG)
    assert tail % SUBLANES == 0 and (tail == 0 or tail // SUBLANES >= CONV_W - 1)
    xy3 = xy.reshape(nseq, seq, 2 * LRU_W)
    vec = lambda: pl.BlockSpec((2, 1, LRU_BLK), lambda b, c: (0, 0, c))
    return pl.pallas_call(
        functools.partial(_lru_kernel, seq=seq),
        grid=(nseq, LRU_BLOCKS),
        in_specs=[
            pl.BlockSpec((1, seq, LRU_BLK), lambda b, c: (b, 0, c)),
            pl.BlockSpec((1, seq, LRU_BLK), lambda b, c: (b, 0, LRU_BLOCKS + c)),
            pl.BlockSpec((2, CONV_W, LRU_BLK), lambda b, c: (0, 0, c)),
            vec(),
            pl.BlockSpec((2, 1, LRU_BLK, 2 * LRU_BLK), lambda b, c: (0, c, 0, 0)),
            vec(), vec(), vec(),
        ],
        out_specs=pl.BlockSpec((1, seq, LRU_BLK), lambda b, c: (b, 0, c)),
        out_shape=jax.ShapeDtypeStruct((nseq, seq, LRU_W), F32),
        compiler_params=_params(("arbitrary", "arbitrary")),
    )(xy3, xy3, cw, cb.reshape(2, 1, LRU_W), w, ba.reshape(2, 1, LRU_W),
      bx.reshape(2, 1, LRU_W), lam.reshape(2, 1, LRU_W))


def _outproj_kernel(attn_ref, lru_ref, x_ref, ga_ref, gl_ref, w_ref, g2_ref, wr_ref, x1_ref, h2_ref, aff_ref):
    an = _rms(attn_ref[...], ga_ref[...]).astype(BF16)
    ln = _rms(lru_ref[...], gl_ref[...]).astype(BF16)
    y = jnp.dot(an, w_ref[:ATTN_W], preferred_element_type=F32)
    y = y + jnp.dot(ln, w_ref[ATTN_W:], preferred_element_type=F32)
    x1 = x_ref[...] + y
    x1_ref[...] = x1
    hf = _rms(x1, g2_ref[...])
    h_hi = hf.astype(BF16)
    h2_ref[...] = h_hi
    h_lo = (hf - h_hi.astype(F32)).astype(BF16)
    nt = (((1,), (1,)), ((), ()))
    a = lax.dot_general(wr_ref[...], h_hi, nt, preferred_element_type=F32)
    b = lax.dot_general(wr_ref[:N_EXPERTS], h_lo, nt, preferred_element_type=F32)
    logits = a[:N_EXPERTS] + a[N_EXPERTS:] + b
    e = jnp.exp(logits - jnp.max(logits, axis=0, keepdims=True))
    aff_ref[...] = e / jnp.sum(e, axis=0, keepdims=True)


def _outproj(attn, lru, x, ga, gl, w, l, g2, wr_t):
    n = x.shape[0]
    row = lambda w_: pl.BlockSpec((ROW_TILE, w_), lambda i: (i, 0))
    full = lambda a: pl.BlockSpec(a.shape, lambda i: (0,) * a.ndim)
    return pl.pallas_call(
        _outproj_kernel,
        grid=(n // ROW_TILE,),
        in_specs=[row(ATTN_W), row(LRU_W), row(D_MODEL), full(ga), full(gl),
                  _layer_weight(w, l), full(g2), full(wr_t)],
        out_specs=[row(D_MODEL), row(D_MODEL), pl.BlockSpec((N_EXPERTS, ROW_TILE), lambda i: (0, i))],
        out_shape=[
            jax.ShapeDtypeStruct((n, D_MODEL), F32),
            jax.ShapeDtypeStruct((n, D_MODEL), BF16),
            jax.ShapeDtypeStruct((N_EXPERTS, n), F32),
        ],
        compiler_params=_params(("arbitrary",)),
    )(attn, lru, x, ga, gl, w, g2, wr_t)


def _split_router(w_router):
    wt = w_router.astype(F32).T
    hi = wt.astype(BF16)
    lo = (wt - hi.astype(F32)).astype(BF16)
    return jnp.concatenate([hi, lo], axis=0)


def _threshold_kernel(aff_ref, thr_ref, need_ref, gt_ref, eq_ref, *, groups):
    for gi, (lo, hi, cap) in enumerate(groups):
        def keys():
            return lax.bitcast_convert_type(aff_ref[:, lo:hi], I32)

        def step(it, thr):
            cand = thr | jnp.left_shift(jnp.int32(1), 30 - it)
            cnt = jnp.sum((keys() >= cand).astype(F32), axis=1, keepdims=True)
            return jnp.where(cnt >= cap, cand, thr)

        thr = lax.fori_loop(0, 31, step, jnp.zeros((N_EXPERTS, 1), I32))
        ngt = jnp.sum((keys() > thr).astype(F32), axis=1, keepdims=True)
        thr_ref[gi] = jnp.broadcast_to(thr, (N_EXPERTS, LANES))
        need_ref[gi] = jnp.broadcast_to(cap - ngt, (N_EXPERTS, LANES))
        for t in range(lo // ROUTE_TILE, hi // ROUTE_TILE):
            kt = lax.bitcast_convert_type(aff_ref[:, t * ROUTE_TILE:(t + 1) * ROUTE_TILE], I32)
            gt_ref[t] = jnp.broadcast_to(jnp.sum((kt > thr).astype(F32), axis=1, keepdims=True), (N_EXPERTS, LANES))
            eq_ref[t] = jnp.broadcast_to(jnp.sum((kt == thr).astype(F32), axis=1, keepdims=True), (N_EXPERTS, LANES))


def _thresholds(aff_t, groups):
    n = aff_t.shape[1]
    nt = n // ROUTE_TILE
    ng = len(groups)
    return pl.pallas_call(
        functools.partial(_threshold_kernel, groups=groups),
        out_shape=[
            jax.ShapeDtypeStruct((ng, N_EXPERTS, LANES), I32),
            jax.ShapeDtypeStruct((ng, N_EXPERTS, LANES), F32),
            jax.ShapeDtypeStruct((nt, N_EXPERTS, LANES), F32),
            jax.ShapeDtypeStruct((nt, N_EXPERTS, LANES), F32),
        ],
        compiler_params=pltpu.CompilerParams(vmem_limit_bytes=VMEM_LIMIT),
    )(aff_t)


def _segment_pieces(tile, cpad_s, off_s, fn):
    for e in range(N_EXPERTS):
        npiece = cpad_s[tile * N_EXPERTS + e] // PIECE
        first = off_s[tile * N_EXPERTS + e]

        def body(j, _, e=e, first=first):
            fn(e, j, pl.multiple_of(first + j * PIECE, PIECE))
            return 0

        lax.fori_loop(0, npiece, body, 0)


def _piece_onehot(gpos, e, row):
    rid = lax.broadcasted_iota(I32, (PIECE, gpos.shape[1]), 0) + row
    return gpos[e:e + 1, :] == rid


def _dispatch_kernel(base_s, cpad_s, off_s, nch_s, tail_s,
                     h2_ref, aff_ref, thr_ref, need_ref, ceq_ref, offv_ref,
                     xe_ref, gpos_ref, stage, pbuf, zeros, sem, *, tile_group, cap_rows):
    i = pl.program_id(0)
    tm = ROUTE_TILE

    @pl.when(i == 0)
    def _():
        pbuf[...] = jnp.zeros(pbuf.shape, BF16)

    grp = tile_group(i)
    thr = jnp.where(grp == 0, thr_ref[0], thr_ref[1])[:, :1]
    need = jnp.where(grp == 0, need_ref[0], need_ref[1])[:, :1]
    aff = aff_ref[...]
    keys = lax.bitcast_convert_type(aff, I32)
    gt = keys > thr
    eq = keys == thr
    before = (lax.broadcasted_iota(I32, (tm, tm), 0) < lax.broadcasted_iota(I32, (tm, tm), 1))
    before = jnp.where(before, 1.0, 0.0).astype(BF16)
    eq_rank = jnp.dot(jnp.where(eq, 1.0, 0.0).astype(BF16), before, preferred_element_type=F32)
    sel = gt | (eq & (ceq_ref[0][:, :1] + eq_rank < need))
    rank = jnp.dot(jnp.where(sel, 1.0, 0.0).astype(BF16), before, preferred_element_type=F32)
    gpos = jnp.where(sel, offv_ref[0][:, :1] + rank.astype(I32), -1)
    gpos_ref[...] = gpos

    g_hi = aff.astype(BF16).astype(F32)
    g_mid = (aff - g_hi).astype(BF16).astype(F32)
    g_lo = (aff - g_hi - g_mid).astype(BF16).astype(F32)
    pad = jnp.zeros((GATE_COLS - 3 * N_EXPERTS, tm), F32)
    gate_cols = jnp.concatenate([g_hi, g_mid, g_lo, pad], axis=0).T.astype(BF16)

    def build(e, j, row):
        pbuf[pl.ds(row, PIECE), :] = jnp.where(_piece_onehot(gpos, e, row), 1.0, 0.0).astype(BF16)

    _segment_pieces(i, cpad_s, off_s, build)

    def chunk(c, _):
        r0 = pl.multiple_of(c * MXU_DIM, MXU_DIM)
        onehot = pbuf[pl.ds(r0, MXU_DIM), :]
        rows = jnp.dot(onehot, h2_ref[...], preferred_element_type=F32)
        stage[pl.ds(r0, MXU_DIM), pl.ds(0, D_MODEL)] = rows.astype(BF16)
        gates = jnp.dot(onehot, gate_cols, preferred_element_type=F32)
        stage[pl.ds(r0, MXU_DIM), pl.ds(D_MODEL, GATE_COLS)] = gates.astype(BF16)
        return 0

    lax.fori_loop(0, nch_s[i], chunk, 0)

    def copy(e, j, row):
        dst = pl.multiple_of(e * cap_rows + base_s[i * N_EXPERTS + e] + j * PIECE, PIECE)
        return pltpu.make_async_copy(stage.at[pl.ds(row, PIECE)], xe_ref.at[pl.ds(dst, PIECE)], sem)

    _segment_pieces(i, cpad_s, off_s, lambda e, j, row: copy(e, j, row).start())
    _segment_pieces(i, cpad_s, off_s, lambda e, j, row: copy(e, j, row).wait())

    @pl.when(i == pl.num_programs(0) - 1)
    def _():
        zeros[...] = jnp.zeros(zeros.shape, BF16)
        for e in range(N_EXPERTS):
            def mk(j, e=e):
                dst = pl.multiple_of(e * cap_rows + tail_s[e] + j * PIECE, PIECE)
                return pltpu.make_async_copy(zeros, xe_ref.at[pl.ds(dst, PIECE)], sem)

            def go(j, _, mk=mk):
                mk(j).start()
                return 0

            def done(j, _, mk=mk):
                mk(j).wait()
                return 0

            ntail = tail_s[N_EXPERTS + e]
            lax.fori_loop(0, ntail, go, 0)
            lax.fori_loop(0, ntail, done, 0)


def _dispatch(meta, h2, aff_t, thr, need, tile_group, cap_rows):
    n = h2.shape[0]
    nt = n // ROUTE_TILE
    tile3 = lambda: pl.BlockSpec((1, N_EXPERTS, LANES), lambda i, *_: (i, 0, 0))
    full3 = lambda a: pl.BlockSpec(a.shape, lambda i, *_: (0, 0, 0))
    grid_spec = pltpu.PrefetchScalarGridSpec(
        num_scalar_prefetch=5,
        grid=(nt,),
        in_specs=[
            pl.BlockSpec((ROUTE_TILE, D_MODEL), lambda i, *_: (i, 0)),
            pl.BlockSpec((N_EXPERTS, ROUTE_TILE), lambda i, *_: (0, i)),
            full3(thr), full3(need), tile3(), tile3(),
        ],
        out_specs=[
            pl.BlockSpec(memory_space=pl.ANY),
            pl.BlockSpec((N_EXPERTS, ROUTE_TILE), lambda i, *_: (0, i)),
        ],
        scratch_shapes=[
            pltpu.VMEM((N_EXPERTS * ROUTE_TILE, XE_W), BF16),
            pltpu.VMEM((N_EXPERTS * ROUTE_TILE, ROUTE_TILE), BF16),
            pltpu.VMEM((PIECE, XE_W), BF16),
            pltpu.SemaphoreType.DMA,
        ],
    )
    return pl.pallas_call(
        functools.partial(_dispatch_kernel, tile_group=tile_group, cap_rows=cap_rows),
        grid_spec=grid_spec,
        out_shape=[
            jax.ShapeDtypeStruct((N_EXPERTS * cap_rows, XE_W), BF16),
            jax.ShapeDtypeStruct((N_EXPERTS, n), I32),
        ],
        compiler_params=_params(("arbitrary",)),
    )(meta["base"], meta["cpad"], meta["off"], meta["nch"], meta["tail"],
      h2, aff_t, thr, need, meta["ceq_v"], meta["off_v"])


def _expert_kernel(used_s, xe_ref, wg_ref, wu_ref, wd_ref, ye_ref):
    e = pl.program_id(0)
    r = pl.program_id(1)

    @pl.when(r * EXPERT_TILE < used_s[e])
    def _():
        x = xe_ref[:, :D_MODEL]
        gc = xe_ref[:, D_MODEL:].astype(F32)
        lane = lax.broadcasted_iota(I32, gc.shape, 1)
        gate = jnp.sum(jnp.where((lane & (N_EXPERTS - 1)) == e, gc, 0.0), axis=1, keepdims=True)
        hg = jnp.dot(x, wg_ref[0], preferred_element_type=F32)
        hu = jnp.dot(x, wu_ref[0], preferred_element_type=F32)
        hid = (jax.nn.silu(hg) * hu).astype(BF16)
        y = jnp.dot(hid, wd_ref[0], preferred_element_type=F32) * gate
        ye_ref[...] = y.astype(BF16)


def _experts(used, xe, wg, wu, wd, l, cap_rows):
    rt = cap_rows // EXPERT_TILE

    def rows(e, r, used_s):
        last = jnp.maximum((used_s[e] + EXPERT_TILE - 1) // EXPERT_TILE - 1, 0)
        return (e * rt + jnp.minimum(r, last), 0)

    weight = lambda w: pl.BlockSpec((None, 1) + w.shape[2:], lambda e, r, u: (l, e, 0, 0))
    grid_spec = pltpu.PrefetchScalarGridSpec(
        num_scalar_prefetch=1,
        grid=(N_EXPERTS, rt),
        in_specs=[pl.BlockSpec((EXPERT_TILE, XE_W), rows), weight(wg), weight(wu), weight(wd)],
        out_specs=pl.BlockSpec((EXPERT_TILE, D_MODEL), rows),
    )
    return pl.pallas_call(
        _expert_kernel,
        grid_spec=grid_spec,
        out_shape=jax.ShapeDtypeStruct((N_EXPERTS * cap_rows, D_MODEL), BF16),
        compiler_params=_params(("arbitrary", "arbitrary")),
    )(used, xe, wg, wu, wd)


def _combine_kernel(base_s, cpad_s, off_s, nch_s, rall_s, x1_ref, gpos_ref, ye_ref, x2_ref,
                    stage, pbuf, sem, *, cap_rows):
    i = pl.program_id(0)

    @pl.when(i == 0)
    def _():
        stage[...] = jnp.zeros(stage.shape, BF16)

    gpos = gpos_ref[...]

    def copy(e, j, row):
        src = pl.multiple_of(e * cap_rows + base_s[i * N_EXPERTS + e] + j * PIECE, PIECE)
        return pltpu.make_async_copy(ye_ref.at[pl.ds(src, PIECE)], stage.at[pl.ds(row, PIECE)], sem)

    def fetch(e, j, row):
        copy(e, j, row).start()
        pbuf[pl.ds(row, PIECE), :] = jnp.where(_piece_onehot(gpos, e, row), 1.0, 0.0).astype(BF16)

    _segment_pieces(i, cpad_s, off_s, fetch)

    def clear(j, _):
        row = pl.multiple_of(rall_s[i] + j * PIECE, PIECE)
        pbuf[pl.ds(row, PIECE), :] = jnp.zeros((PIECE, ROUTE_TILE), BF16)
        return 0

    lax.fori_loop(0, (nch_s[i] * MXU_DIM - rall_s[i]) // PIECE, clear, 0)
    x2_ref[...] = x1_ref[...]
    _segment_pieces(i, cpad_s, off_s, lambda e, j, row: copy(e, j, row).wait())

    def chunk(c, _):
        r0 = pl.multiple_of(c * MXU_DIM, MXU_DIM)
        onehot_t = pbuf[pl.ds(r0, MXU_DIM), :].astype(F32).T.astype(BF16)
        x2_ref[...] += jnp.dot(onehot_t, stage[pl.ds(r0, MXU_DIM), :], preferred_element_type=F32)
        return 0

    lax.fori_loop(0, nch_s[i], chunk, 0)


def _combine(meta, x1, gpos, ye, cap_rows):
    n = x1.shape[0]
    grid_spec = pltpu.PrefetchScalarGridSpec(
        num_scalar_prefetch=5,
        grid=(n // ROUTE_TILE,),
        in_specs=[
            pl.BlockSpec((ROUTE_TILE, D_MODEL), lambda i, *_: (i, 0)),
            pl.BlockSpec((N_EXPERTS, ROUTE_TILE), lambda i, *_: (0, i)),
            pl.BlockSpec(memory_space=pl.ANY),
        ],
        out_specs=pl.BlockSpec((ROUTE_TILE, D_MODEL), lambda i, *_: (i, 0)),
        scratch_shapes=[
            pltpu.VMEM((N_EXPERTS * ROUTE_TILE, D_MODEL), BF16),
            pltpu.VMEM((N_EXPERTS * ROUTE_TILE, ROUTE_TILE), BF16),
            pltpu.SemaphoreType.DMA,
        ],
    )
    return pl.pallas_call(
        functools.partial(_combine_kernel, cap_rows=cap_rows),
        grid_spec=grid_spec,
        out_shape=jax.ShapeDtypeStruct((n, D_MODEL), F32),
        compiler_params=_params(("arbitrary",)),
    )(meta["base"], meta["cpad"], meta["off"], meta["nch"], meta["rall"], x1, gpos, ye)


def _final_kernel(x_ref, g_ref, op_ref, os_ref, *, tiles_p):
    i = pl.program_id(0)
    y = _rms(x_ref[...], g_ref[...])

    @pl.when(i < tiles_p)
    def _():
        op_ref[...] = y

    @pl.when(i >= tiles_p)
    def _():
        os_ref[...] = y


def _final_norm(x, g, n_p):
    n = x.shape[0]
    tiles_p = n_p // ROW_TILE
    return pl.pallas_call(
        functools.partial(_final_kernel, tiles_p=tiles_p),
        grid=(n // ROW_TILE,),
        in_specs=[pl.BlockSpec((ROW_TILE, D_MODEL), lambda i: (i, 0)),
                  pl.BlockSpec((1, D_MODEL), lambda i: (0, 0))],
        out_specs=[
            pl.BlockSpec((ROW_TILE, D_MODEL), lambda i: (jnp.minimum(i, tiles_p - 1), 0)),
            pl.BlockSpec((ROW_TILE, D_MODEL), lambda i: (jnp.maximum(i - tiles_p, 0), 0)),
        ],
        out_shape=[
            jax.ShapeDtypeStruct((n_p, D_MODEL), F32),
            jax.ShapeDtypeStruct((n - n_p, D_MODEL), F32),
        ],
        compiler_params=_params(("arbitrary",)),
    )(x, g)


def _route_meta(gtc, eqc, need, tile_grp, group_first_tile, cap_rows):
    gtc = gtc[:, :, 0].astype(I32)
    eqc = eqc[:, :, 0].astype(I32)
    need_t = need[:, :, 0].astype(I32)[tile_grp]
    ceq = jnp.cumsum(eqc, axis=0) - eqc
    ceq = ceq - ceq[group_first_tile]
    cnt = gtc + jnp.clip(need_t - ceq, 0, eqc)
    cpad = (cnt + PIECE - 1) // PIECE * PIECE
    base = jnp.cumsum(cpad, axis=0) - cpad
    off = jnp.cumsum(cpad, axis=1) - cpad
    used = jnp.sum(cpad, axis=0)
    rall = jnp.sum(cpad, axis=1)
    nch = (rall + MXU_DIM - 1) // MXU_DIM
    tail_n = ((used + EXPERT_TILE - 1) // EXPERT_TILE * EXPERT_TILE - used) // PIECE
    bcast = lambda a, dt: jnp.broadcast_to(a[:, :, None].astype(dt), a.shape + (LANES,))
    return {
        "base": base.reshape(-1), "cpad": cpad.reshape(-1), "off": off.reshape(-1), "nch": nch, "rall": rall,
        "used": used, "tail": jnp.concatenate([used, tail_n]),
        "ceq_v": bcast(ceq, F32), "off_v": bcast(off, I32),
    }


def kernel(x_prompt, x_sample, rel_bias, norm1_g, w_in, attn_sink, conv_w, conv_b, w_rec_a, b_rec_a,
           w_rec_x, b_rec_x, lru_lambda, grp_g_attn, grp_g_lru, w_out, norm2_g, w_router,
           w_e_gate, w_e_up, w_e_down, final_g):
    bp, seq, _ = x_prompt.shape
    bs = x_sample.shape[0]
    assert x_sample.shape[1] == seq
    nseq = bp + bs
    n_p, n_s = bp * seq, bs * seq
    n = n_p + n_s
    assert seq % ATTN_TQ == 0 and n_p % ROW_TILE == 0 and n_s % ROW_TILE == 0
    assert n_p % ROUTE_TILE == 0 and n_s % ROUTE_TILE == 0
    depth = w_in.shape[0]

    groups = ((0, n_p, EC_CAPACITY * n_p // N_EXPERTS), (n_p, n, EC_CAPACITY * n_s // N_EXPERTS))
    nt = n // ROUTE_TILE
    nt_p = n_p // ROUTE_TILE
    tile_grp = np.where(np.arange(nt) < nt_p, 0, 1)
    group_first_tile = np.where(np.arange(nt) < nt_p, 0, nt_p)
    real_rows = groups[0][2] + groups[1][2]
    cap_rows = -(-(real_rows + (PIECE - 1) * nt) // EXPERT_TILE) * EXPERT_TILE
    tile_group = lambda i: (i >= nt_p).astype(I32)

    bias_tbl = _bias_table(rel_bias)
    row = lambda v: v.reshape(1, -1).astype(F32)
    x = None
    w_in_b, w_out_b = w_in.astype(BF16), w_out.astype(BF16)
    w_gate_b, w_up_b, w_down_b = w_e_gate.astype(BF16), w_e_up.astype(BF16), w_e_down.astype(BF16)

    for l in range(depth):
        if l == 0:
            qkv, xy, x = _inproj_first(x_prompt.reshape(n_p, D_MODEL), x_sample.reshape(n_s, D_MODEL),
                                       row(norm1_g[l]), w_in_b)
        else:
            qkv, xy = _inproj(x, row(norm1_g[l]), w_in_b, l)
        sink_tbl = jnp.broadcast_to(attn_sink[l].astype(F32)[:, None, None], (N_HEADS, 1, LANES))
        attn = _attention(qkv, bias_tbl, sink_tbl, nseq, seq).reshape(n, ATTN_W)
        w_rec = jnp.concatenate([w_rec_a[l], w_rec_x[l]], axis=-1).astype(BF16)
        lru = _lru(xy, conv_w[l].astype(F32), conv_b[l].astype(F32), w_rec, b_rec_a[l].astype(F32),
                   b_rec_x[l].astype(F32), lru_lambda[l].astype(F32), nseq, seq).reshape(n, LRU_W)
        x1, h2, aff_t = _outproj(attn, lru, x, row(grp_g_attn[l]), row(grp_g_lru[l]),
                                 w_out_b, l, row(norm2_g[l]), _split_router(w_router[l]))
        thr, need, gtc, eqc = _thresholds(aff_t, groups)
        meta = _route_meta(gtc, eqc, need, tile_grp, group_first_tile, cap_rows)
        xe, gpos = _dispatch(meta, h2, aff_t, thr, need, tile_group, cap_rows)
        ye = _experts(meta["used"], xe, w_gate_b, w_up_b, w_down_b, l, cap_rows)
        x = _combine(meta, x1, gpos, ye, cap_rows)

    y_p, y_s = _final_norm(x, row(final_g), n_p)
    return (y_p.reshape(bp, seq, D_MODEL), y_s.reshape(bs, seq, D_MODEL))
```

```python
import functools

import numpy as np
import jax
import jax.numpy as jnp
from jax import lax
from jax.experimental import pallas as pl
from jax.experimental.pallas import tpu as pltpu

F32 = jnp.float32
BF16 = jnp.bfloat16
I32 = jnp.int32

D_MODEL = 2048
HEAD_DIM = 128
N_HEADS = 8
N_KV_HEADS = 2
GQA_GROUP = N_HEADS // N_KV_HEADS
ATTN_W = N_HEADS * HEAD_DIM
KV_W = N_KV_HEADS * HEAD_DIM
QKV_W = ATTN_W + 2 * KV_W
WINDOW = 128
BLOCK = 128
N_BUCKETS = 32
MAX_DISTANCE = 128
LRU_W = D_MODEL - ATTN_W
LRU_BLOCKS = 8
LRU_BLK = LRU_W // LRU_BLOCKS
CONV_W = 4
LRU_C = 8.0
N_EXPERTS = 16
EC_CAPACITY = 2
D_EXPERT = D_MODEL // 2
EPS = 1e-6
NEG = -1e30

LANES = 128
SUBLANES = 8
BF16_ROWS = 16
MXU_DIM = 256
VMEM_LIMIT = 56 * 1024 * 1024

ROW_TILE = 512
ATTN_TQ = 512
ATTN_SKEW = 2
LRU_SEG = 60
ROUTE_TILE = 256
PIECE = BF16_ROWS
EXPERT_TILE = 512
GATE_COLS = LANES
XE_W = D_MODEL + GATE_COLS


def _params(sem, **kw):
    return pltpu.CompilerParams(dimension_semantics=sem, vmem_limit_bytes=VMEM_LIMIT, **kw)


def _rms(x, g):
    return x * lax.rsqrt(jnp.mean(x * x, axis=-1, keepdims=True) + EPS) * g


def _inproj_kernel(x_ref, g_ref, w_ref, qkv_ref, xy_ref):
    h = _rms(x_ref[...], g_ref[...]).astype(BF16)
    qkv_ref[...] = jnp.dot(h, w_ref[:, :QKV_W], preferred_element_type=F32).astype(BF16)
    xy_ref[...] = jnp.dot(h, w_ref[:, QKV_W:], preferred_element_type=F32)


def _layer_weight(w, l):
    return pl.BlockSpec((None,) + w.shape[1:], lambda i: (l, 0, 0), pipeline_mode=pl.Buffered(1))


def _inproj(x, g, w, l):
    n = x.shape[0]
    return pl.pallas_call(
        _inproj_kernel,
        grid=(n // ROW_TILE,),
        in_specs=[
            pl.BlockSpec((ROW_TILE, D_MODEL), lambda i: (i, 0)),
            pl.BlockSpec((1, D_MODEL), lambda i: (0, 0)),
            _layer_weight(w, l),
        ],
        out_specs=[
            pl.BlockSpec((ROW_TILE, QKV_W), lambda i: (i, 0)),
            pl.BlockSpec((ROW_TILE, 2 * LRU_W), lambda i: (i, 0)),
        ],
        out_shape=[
            jax.ShapeDtypeStruct((n, QKV_W), BF16),
            jax.ShapeDtypeStruct((n, 2 * LRU_W), F32),
        ],
        compiler_params=_params(("arbitrary",)),
    )(x, g, w)


def _inproj_first_kernel(xp_ref, xs_ref, g_ref, w_ref, qkv_ref, xy_ref, x_ref, *, tiles_p):
    i = pl.program_id(0)

    def run(src):
        x_ref[...] = src[...]
        _inproj_kernel(src, g_ref, w_ref, qkv_ref, xy_ref)

    pl.when(i < tiles_p)(lambda: run(xp_ref))
    pl.when(i >= tiles_p)(lambda: run(xs_ref))


def _inproj_first(xp, xs, g, w):
    n_p, n_s = xp.shape[0], xs.shape[0]
    n = n_p + n_s
    tile = ROW_TILE // 2
    tiles_p = n_p // tile
    row = lambda w_: pl.BlockSpec((tile, w_), lambda i: (i, 0))
    return pl.pallas_call(
        functools.partial(_inproj_first_kernel, tiles_p=tiles_p),
        grid=(n // tile,),
        in_specs=[
            pl.BlockSpec((tile, D_MODEL), lambda i: (jnp.minimum(i, tiles_p - 1), 0)),
            pl.BlockSpec((tile, D_MODEL), lambda i: (jnp.maximum(i - tiles_p, 0), 0)),
            pl.BlockSpec((1, D_MODEL), lambda i: (0, 0)),
            _layer_weight(w, 0),
        ],
        out_specs=[row(QKV_W), row(2 * LRU_W), row(D_MODEL)],
        out_shape=[
            jax.ShapeDtypeStruct((n, QKV_W), BF16),
            jax.ShapeDtypeStruct((n, 2 * LRU_W), F32),
            jax.ShapeDtypeStruct((n, D_MODEL), F32),
        ],
        compiler_params=_params(("arbitrary",)),
    )(xp, xs, g, w)


def _attn_kernel(q_ref, kp_ref, km_ref, kn_ref, vp_ref, vm_ref, vn_ref, bias_ref, sink_ref, o_ref,
                 s_scr, p_scr, m_scr, d_scr, *, seq):
    i = pl.program_id(2)
    kwin = jnp.concatenate([kp_ref[0], km_ref[0], kn_ref[0]], axis=0)
    vwin = jnp.concatenate([vp_ref[0], vm_ref[0], vn_ref[0]], axis=0)
    qi = lax.broadcasted_iota(I32, (BLOCK, 3 * BLOCK), 0)
    ji = lax.broadcasted_iota(I32, (BLOCK, 3 * BLOCK), 1)
    band = jnp.abs(ji - BLOCK - qi) <= WINDOW
    scale = HEAD_DIM ** -0.5
    nsb = ATTN_TQ // BLOCK
    unit = lambda sb, g: sb * GQA_GROUP + g
    qrows = lambda sb: slice(sb * BLOCK, (sb + 1) * BLOCK)
    hcols = lambda g: slice(g * HEAD_DIM, (g + 1) * HEAD_DIM)
    sink = [sink_ref[g][:, :1] for g in range(GQA_GROUP)]
    ones = jnp.ones((3 * BLOCK, HEAD_DIM), BF16)
    valid, vext = [], []
    for sb in range(nsb):
        kpos = i * ATTN_TQ + (sb - 1) * BLOCK + ji
        valid.append(band & (kpos >= 0) & (kpos < seq))
        vext.append(jnp.concatenate([vwin[sb * BLOCK:sb * BLOCK + 3 * BLOCK], ones], axis=1))

    def scores(sb, g):
        s = lax.dot_general(q_ref[0, qrows(sb), hcols(g)], kwin[sb * BLOCK:sb * BLOCK + 3 * BLOCK],
                            (((1,), (1,)), ((), ())), preferred_element_type=F32)
        s_scr[unit(sb, g)] = jnp.where(valid[sb], s * scale + bias_ref[g], NEG)

    def row_max(sb, g):
        u = unit(sb, g)
        m_scr[u] = jnp.maximum(jnp.max(s_scr[u], axis=-1, keepdims=True), sink[g])

    def exponent(sb, g):
        u = unit(sb, g)
        p_scr[u] = jnp.exp(s_scr[u] - m_scr[u]).astype(BF16)

    def values(sb, g):
        u = unit(sb, g)
        o = jnp.dot(p_scr[u], vext[sb], preferred_element_type=F32)
        denom = o[:, HEAD_DIM:HEAD_DIM + 1] + jnp.exp(sink[g] - m_scr[u])
        o_ref[0, qrows(sb), hcols(g)] = o[:, :HEAD_DIM] / denom

    passes = (scores, row_max, exponent, values)
    units = [(sb, g) for sb in range(nsb) for g in range(GQA_GROUP)]
    for t in range(len(units) + ATTN_SKEW * (len(passes) - 1)):
        for k, fn in enumerate(passes):
            if 0 <= t - ATTN_SKEW * k < len(units):
                fn(*units[t - ATTN_SKEW * k])


def _attention(qkv, bias_tbl, sink_tbl, nseq, seq):
    qkv3 = qkv.reshape(nseq, seq, QKV_W)
    per = ATTN_TQ // BLOCK
    last = seq // BLOCK - 1
    kcol = ATTN_W // HEAD_DIM
    vcol = (ATTN_W + KV_W) // HEAD_DIM

    def side(col, prev):
        if prev:
            return pl.BlockSpec((1, BLOCK, HEAD_DIM), lambda b, k, i: (b, jnp.maximum(i * per - 1, 0), col + k))
        return pl.BlockSpec((1, BLOCK, HEAD_DIM), lambda b, k, i: (b, jnp.minimum(i * per + per, last), col + k))

    def main(col):
        return pl.BlockSpec((1, ATTN_TQ, HEAD_DIM), lambda b, k, i: (b, i, col + k))

    qw = GQA_GROUP * HEAD_DIM
    return pl.pallas_call(
        functools.partial(_attn_kernel, seq=seq),
        grid=(nseq, N_KV_HEADS, seq // ATTN_TQ),
        in_specs=[
            pl.BlockSpec((1, ATTN_TQ, qw), lambda b, k, i: (b, i, k)),
            side(kcol, True), main(kcol), side(kcol, False),
            side(vcol, True), main(vcol), side(vcol, False),
            pl.BlockSpec((GQA_GROUP, BLOCK, 3 * BLOCK), lambda b, k, i: (k, 0, 0)),
            pl.BlockSpec((GQA_GROUP, 1, LANES), lambda b, k, i: (k, 0, 0)),
        ],
        out_specs=pl.BlockSpec((1, ATTN_TQ, qw), lambda b, k, i: (b, i, k)),
        out_shape=jax.ShapeDtypeStruct((nseq, seq, ATTN_W), F32),
        scratch_shapes=[
            pltpu.VMEM((per * GQA_GROUP, BLOCK, 3 * BLOCK), F32),
            pltpu.VMEM((per * GQA_GROUP, BLOCK, 3 * BLOCK), BF16),
            pltpu.VMEM((per * GQA_GROUP, BLOCK, 1), F32),
            pltpu.VMEM((per * GQA_GROUP, BLOCK, 1), F32),
        ],
        compiler_params=_params(("arbitrary", "arbitrary", "arbitrary")),
    )(qkv3, qkv3, qkv3, qkv3, qkv3, qkv3, qkv3, bias_tbl, sink_tbl)


def _t5_bucket(rel):
    nb = N_BUCKETS // 2
    ret = (rel > 0).astype(np.int32) * nb
    n = np.abs(rel)
    max_exact = nb // 2
    large = max_exact + (np.log(np.maximum(n, 1) / max_exact) / np.log(MAX_DISTANCE / max_exact)
                         * (nb - max_exact)).astype(np.int32)
    large = np.minimum(large, nb - 1)
    return (ret + np.where(n < max_exact, n, large)).astype(np.int32)


def _bias_table(rel_bias):
    rel = (np.arange(3 * BLOCK)[None, :] - BLOCK) - np.arange(BLOCK)[:, None]
    onehot = np.eye(N_BUCKETS, dtype=np.float32)[_t5_bucket(rel)]
    return jnp.einsum("qkb,bh->hqk", onehot, rel_bias.astype(F32), precision=lax.Precision.HIGHEST)


def _sublane_scan(a, u, reverse):
    sub = lax.broadcasted_iota(I32, a.shape, 0)
    for d in (1, 2, 4):
        keep = (sub < SUBLANES - d) if reverse else (sub >= d)
        shift = SUBLANES - d if reverse else d
        a_sh = jnp.where(keep, pltpu.roll(a, shift, 0), 1.0)
        u_sh = jnp.where(keep, pltpu.roll(u, shift, 0), 0.0)
        u = a * u_sh + u
        a = a * a_sh
    return a, u


def _lru_chunk(xr_ref, base, seg, d, prm, state):
    cw, cb, w, ba, bx, decay = prm
    halo, carry = state[:3], state[3]
    sub = lax.broadcasted_iota(I32, (SUBLANES, LRU_BLK), 0)
    x = [xr_ref[0, pl.ds(base + j, SUBLANES, stride=seg), :] for j in range(seg)]
    if d == 0:
        edge = [jnp.where(sub == 0, halo[m], pltpu.roll(x[seg - 1 - m], 1, 0)) for m in range(3)]
        at = lambda j: x[j] if j >= 0 else edge[-j - 1]
        taps = lambda j: [at(j - k) for k in range(CONV_W)]
        new_halo = [x[seg - 1 - m][SUBLANES - 1:SUBLANES] for m in range(3)]
    else:
        edge = [jnp.where(sub == SUBLANES - 1, halo[m], pltpu.roll(x[m], SUBLANES - 1, 0)) for m in range(3)]
        at = lambda j: x[j] if j < seg else edge[j - seg]
        taps = lambda j: [at(j + k) for k in range(CONV_W)]
        new_halo = [x[m][0:1] for m in range(3)]
    xc = []
    for j in range(seg):
        t = taps(j)
        acc = cb[d] + cw[d][CONV_W - 1] * t[0]
        for k in range(1, CONV_W):
            acc = acc + cw[d][CONV_W - 1 - k] * t[k]
        xc.append(acc)
    xc_all = jnp.concatenate(xc, axis=0)
    z = jnp.dot(xc_all.astype(BF16), w[d], preferred_element_type=F32)
    r = jax.nn.sigmoid(z[:, :LRU_BLK] + ba[d])
    ig = jax.nn.sigmoid(z[:, LRU_BLK:] + bx[d])
    log_a = decay[d] * r
    a_all = jnp.exp(log_a)
    th = jnp.tanh(log_a)
    u_all = jnp.sqrt(-2.0 * th / (1.0 - th)) * (ig * xc_all)
    a = [a_all[j * SUBLANES:(j + 1) * SUBLANES] for j in range(seg)]
    u = [u_all[j * SUBLANES:(j + 1) * SUBLANES] for j in range(seg)]
    order = list(range(seg)) if d == 0 else list(range(seg - 1, -1, -1))
    hl, pr = [None] * seg, [None] * seg
    prev = None
    for j in order:
        if prev is None:
            hl[j], pr[j] = u[j], a[j]
        else:
            hl[j] = a[j] * hl[prev] + u[j]
            pr[j] = a[j] * pr[prev]
        prev = j
    ai, ui = _sublane_scan(pr[prev], hl[prev], d == 1)
    end = ui + ai * carry
    if d == 0:
        start = jnp.where(sub == 0, carry, pltpu.roll(end, 1, 0))
        new_carry = end[SUBLANES - 1:SUBLANES]
    else:
        start = jnp.where(sub == SUBLANES - 1, carry, pltpu.roll(end, SUBLANES - 1, 0))
        new_carry = end[0:1]
    h = [hl[j] + pr[j] * start for j in range(seg)]
    return h, tuple(new_halo) + (new_carry,)


def _lru_kernel(xr_ref, yr_ref, cw_ref, cb_ref, w_ref, ba_ref, bx_ref, lam_ref, o_ref, *, seq):
    chunk = SUBLANES * LRU_SEG
    nfull = seq // chunk
    tail_seg = (seq - nfull * chunk) // SUBLANES
    prm = (
        [[cw_ref[d, j:j + 1, :] for j in range(CONV_W)] for d in range(2)],
        [cb_ref[d] for d in range(2)],
        [w_ref[d, 0] for d in range(2)],
        [ba_ref[d] for d in range(2)],
        [bx_ref[d] for d in range(2)],
        [-LRU_C * jax.nn.softplus(-lam_ref[d]) for d in range(2)],
    )
    rows = lambda base, j, seg: (0, pl.ds(base + j, SUBLANES, stride=seg), slice(None))

    def forward(base, seg, state):
        h, state = _lru_chunk(xr_ref, base, seg, 0, prm, state)
        for j in range(seg):
            o_ref[rows(base, j, seg)] = h[j]
        return state

    def backward(base, seg, state):
        h, state = _lru_chunk(xr_ref, base, seg, 1, prm, state)
        for j in range(seg):
            o_ref[rows(base, j, seg)] = (o_ref[rows(base, j, seg)] + h[j]) * jax.nn.gelu(yr_ref[rows(base, j, seg)])
        return state

    zero = jnp.zeros((1, LRU_BLK), F32)
    state = lax.fori_loop(0, nfull, lambda k, st: forward(k * chunk, LRU_SEG, st), (zero,) * 4)
    if tail_seg:
        forward(nfull * chunk, tail_seg, state)
    state = (zero,) * 4
    if tail_seg:
        state = backward(nfull * chunk, tail_seg, state)
    lax.fori_loop(0, nfull, lambda k, st: backward((nfull - 1 - k) * chunk, LRU_SEG, st), state)


def _lru(xy, cw, cb, w, ba, bx, lam, nseq, seq):
    tail = seq % (SUBLANES * LRU_SEG)
    assert tail % SUBLANES == 0 and (tail == 0 or tail // SUBLANES >= CONV_W - 1)
    xy3 = xy.reshape(nseq, seq, 2 * LRU_W)
    vec = lambda: pl.BlockSpec((2, 1, LRU_BLK), lambda b, c: (0, 0, c))
    return pl.pallas_call(
        functools.partial(_lru_kernel, seq=seq),
        grid=(nseq, LRU_BLOCKS),
        in_specs=[
            pl.BlockSpec((1, seq, LRU_BLK), lambda b, c: (b, 0, c)),
            pl.BlockSpec((1, seq, LRU_BLK), lambda b, c: (b, 0, LRU_BLOCKS + c)),
            pl.BlockSpec((2, CONV_W, LRU_BLK), lambda b, c: (0, 0, c)),
            vec(),
            pl.BlockSpec((2, 1, LRU_BLK, 2 * LRU_BLK), lambda b, c: (0, c, 0, 0)),
            vec(), vec(), vec(),
        ],
        out_specs=pl.BlockSpec((1, seq, LRU_BLK), lambda b, c: (b, 0, c)),
        out_shape=jax.ShapeDtypeStruct((nseq, seq, LRU_W), F32),
        compiler_params=_params(("arbitrary", "arbitrary")),
    )(xy3, xy3, cw, cb.reshape(2, 1, LRU_W), w, ba.reshape(2, 1, LRU_W),
      bx.reshape(2, 1, LRU_W), lam.reshape(2, 1, LRU_W))


def _outproj_kernel(attn_ref, lru_ref, x_ref, ga_ref, gl_ref, w_ref, g2_ref, wr_ref, x1_ref, h2_ref, aff_ref):
    an = _rms(attn_ref[...], ga_ref[...]).astype(BF16)
    ln = _rms(lru_ref[...], gl_ref[...]).astype(BF16)
    y = jnp.dot(an, w_ref[:ATTN_W], preferred_element_type=F32)
    y = y + jnp.dot(ln, w_ref[ATTN_W:], preferred_element_type=F32)
    x1 = x_ref[...] + y
    x1_ref[...] = x1
    hf = _rms(x1, g2_ref[...])
    h_hi = hf.astype(BF16)
    h2_ref[...] = h_hi
    h_lo = (hf - h_hi.astype(F32)).astype(BF16)
    nt = (((1,), (1,)), ((), ()))
    a = lax.dot_general(wr_ref[...], h_hi, nt, preferred_element_type=F32)
    b = lax.dot_general(wr_ref[:N_EXPERTS], h_lo, nt, preferred_element_type=F32)
    logits = a[:N_EXPERTS] + a[N_EXPERTS:] + b
    e = jnp.exp(logits - jnp.max(logits, axis=0, keepdims=True))
    aff_ref[...] = e / jnp.sum(e, axis=0, keepdims=True)


def _outproj(attn, lru, x, ga, gl, w, l, g2, wr_t):
    n = x.shape[0]
    row = lambda w_: pl.BlockSpec((ROW_TILE, w_), lambda i: (i, 0))
    full = lambda a: pl.BlockSpec(a.shape, lambda i: (0,) * a.ndim)
    return pl.pallas_call(
        _outproj_kernel,
        grid=(n // ROW_TILE,),
        in_specs=[row(ATTN_W), row(LRU_W), row(D_MODEL), full(ga), full(gl),
                  _layer_weight(w, l), full(g2), full(wr_t)],
        out_specs=[row(D_MODEL), row(D_MODEL), pl.BlockSpec((N_EXPERTS, ROW_TILE), lambda i: (0, i))],
        out_shape=[
            jax.ShapeDtypeStruct((n, D_MODEL), F32),
            jax.ShapeDtypeStruct((n, D_MODEL), BF16),
            jax.ShapeDtypeStruct((N_EXPERTS, n), F32),
        ],
        compiler_params=_params(("arbitrary",)),
    )(attn, lru, x, ga, gl, w, g2, wr_t)


def _split_router(w_router):
    wt = w_router.astype(F32).T
    hi = wt.astype(BF16)
    lo = (wt - hi.astype(F32)).astype(BF16)
    return jnp.concatenate([hi, lo], axis=0)


def _threshold_kernel(aff_ref, thr_ref, need_ref, gt_ref, eq_ref, *, groups):
    for gi, (lo, hi, cap) in enumerate(groups):
        def keys():
            return lax.bitcast_convert_type(aff_ref[:, lo:hi], I32)

        def step(it, thr):
            cand = thr | jnp.left_shift(jnp.int32(1), 30 - it)
            cnt = jnp.sum((keys() >= cand).astype(F32), axis=1, keepdims=True)
            return jnp.where(cnt >= cap, cand, thr)

        thr = lax.fori_loop(0, 31, step, jnp.zeros((N_EXPERTS, 1), I32))
        ngt = jnp.sum((keys() > thr).astype(F32), axis=1, keepdims=True)
        thr_ref[gi] = jnp.broadcast_to(thr, (N_EXPERTS, LANES))
        need_ref[gi] = jnp.broadcast_to(cap - ngt, (N_EXPERTS, LANES))
        for t in range(lo // ROUTE_TILE, hi // ROUTE_TILE):
            kt = lax.bitcast_convert_type(aff_ref[:, t * ROUTE_TILE:(t + 1) * ROUTE_TILE], I32)
            gt_ref[t] = jnp.broadcast_to(jnp.sum((kt > thr).astype(F32), axis=1, keepdims=True), (N_EXPERTS, LANES))
            eq_ref[t] = jnp.broadcast_to(jnp.sum((kt == thr).astype(F32), axis=1, keepdims=True), (N_EXPERTS, LANES))


def _thresholds(aff_t, groups):
    n = aff_t.shape[1]
    nt = n // ROUTE_TILE
    ng = len(groups)
    return pl.pallas_call(
        functools.partial(_threshold_kernel, groups=groups),
        out_shape=[
            jax.ShapeDtypeStruct((ng, N_EXPERTS, LANES), I32),
            jax.ShapeDtypeStruct((ng, N_EXPERTS, LANES), F32),
            jax.ShapeDtypeStruct((nt, N_EXPERTS, LANES), F32),
            jax.ShapeDtypeStruct((nt, N_EXPERTS, LANES), F32),
        ],
        compiler_params=pltpu.CompilerParams(vmem_limit_bytes=VMEM_LIMIT),
    )(aff_t)


def _segment_pieces(tile, npiece_s, off_s, fn):
    for e in range(N_EXPERTS):
        npiece = npiece_s[tile * N_EXPERTS + e]
        first = off_s[tile * N_EXPERTS + e]

        def body(j, _, e=e, first=first):
            fn(e, j, pl.multiple_of(first + j * PIECE, PIECE))
            return 0

        lax.fori_loop(0, npiece, body, 0)


def _piece_onehot(gpos, e, row):
    rid = lax.broadcasted_iota(I32, (PIECE, gpos.shape[1]), 0) + row
    return gpos[e:e + 1, :] == rid


def _dispatch_kernel(abase_s, npiece_s, off_s, nch_s, nflush_s, rem_s,
                     h2_ref, aff_ref, thr_ref, need_ref, ceq_ref, offv_ref,
                     xe_ref, gpos_ref, stage, pbuf, carry, zeros, sem, *, tile_group, cap_rows, real_rows):
    i = pl.program_id(0)
    tm = ROUTE_TILE

    @pl.when(i == 0)
    def _():
        pbuf[...] = jnp.zeros(pbuf.shape, BF16)

    grp = tile_group(i)
    thr = jnp.where(grp == 0, thr_ref[0], thr_ref[1])[:, :1]
    need = jnp.where(grp == 0, need_ref[0], need_ref[1])[:, :1]
    aff = aff_ref[...]
    keys = lax.bitcast_convert_type(aff, I32)
    gt = keys > thr
    eq = keys == thr
    before = (lax.broadcasted_iota(I32, (tm, tm), 0) < lax.broadcasted_iota(I32, (tm, tm), 1))
    before = jnp.where(before, 1.0, 0.0).astype(BF16)
    eq_rank = jnp.dot(jnp.where(eq, 1.0, 0.0).astype(BF16), before, preferred_element_type=F32)
    sel = gt | (eq & (ceq_ref[0][:, :1] + eq_rank < need))
    rank = jnp.dot(jnp.where(sel, 1.0, 0.0).astype(BF16), before, preferred_element_type=F32)
    gpos = jnp.where(sel, offv_ref[0][:, :1] + rank.astype(I32), -1)
    gpos_ref[...] = gpos

    g_hi = aff.astype(BF16).astype(F32)
    g_mid = (aff - g_hi).astype(BF16).astype(F32)
    g_lo = (aff - g_hi - g_mid).astype(BF16).astype(F32)
    pad = jnp.zeros((GATE_COLS - 3 * N_EXPERTS, tm), F32)
    gate_cols = jnp.concatenate([g_hi, g_mid, g_lo, pad], axis=0).T.astype(BF16)

    def build(e, j, row):
        pbuf[pl.ds(row, PIECE), :] = jnp.where(_piece_onehot(gpos, e, row), 1.0, 0.0).astype(BF16)

    _segment_pieces(i, npiece_s, off_s, build)

    def chunk(c, _):
        r0 = pl.multiple_of(c * MXU_DIM, MXU_DIM)
        onehot = pbuf[pl.ds(r0, MXU_DIM), :]
        rows = jnp.dot(onehot, h2_ref[...], preferred_element_type=F32)
        stage[pl.ds(r0, MXU_DIM), pl.ds(0, D_MODEL)] = rows.astype(BF16)
        gates = jnp.dot(onehot, gate_cols, preferred_element_type=F32)
        stage[pl.ds(r0, MXU_DIM), pl.ds(D_MODEL, GATE_COLS)] = gates.astype(BF16)
        return 0

    lax.fori_loop(0, nch_s[i], chunk, 0)

    for e in range(N_EXPERTS):
        k = i * N_EXPERTS + e

        @pl.when((npiece_s[k] > 0) & (rem_s[k] > 0))
        def _(e=e, k=k):
            head = pl.ds(pl.multiple_of(off_s[k], PIECE), PIECE)
            stage[head, :] = (stage[head, :].astype(F32) + carry[e].astype(F32)).astype(BF16)

    def copy(e, j, row):
        dst = pl.multiple_of(e * cap_rows + abase_s[i * N_EXPERTS + e] + j * PIECE, PIECE)
        return pltpu.make_async_copy(stage.at[pl.ds(row, PIECE)], xe_ref.at[pl.ds(dst, PIECE)], sem)

    _segment_pieces(i, nflush_s, off_s, lambda e, j, row: copy(e, j, row).start())

    for e in range(N_EXPERTS):
        k = i * N_EXPERTS + e

        @pl.when(npiece_s[k] > nflush_s[k])
        def _(e=e, k=k):
            carry[e] = stage[pl.ds(pl.multiple_of(off_s[k] + nflush_s[k] * PIECE, PIECE), PIECE), :]

    _segment_pieces(i, nflush_s, off_s, lambda e, j, row: copy(e, j, row).wait())

    ntail = (cap_rows - real_rows) // PIECE
    if ntail:
        @pl.when(i == pl.num_programs(0) - 1)
        def _():
            zeros[...] = jnp.zeros(zeros.shape, BF16)
            fills = [pltpu.make_async_copy(zeros, xe_ref.at[pl.ds(e * cap_rows + real_rows + j * PIECE, PIECE)], sem)
                     for e in range(N_EXPERTS) for j in range(ntail)]
            for f in fills:
                f.start()
            for f in fills:
                f.wait()


def _dispatch(meta, h2, aff_t, thr, need, tile_group, cap_rows, real_rows):
    n = h2.shape[0]
    nt = n // ROUTE_TILE
    tile3 = lambda: pl.BlockSpec((1, N_EXPERTS, LANES), lambda i, *_: (i, 0, 0))
    full3 = lambda a: pl.BlockSpec(a.shape, lambda i, *_: (0, 0, 0))
    grid_spec = pltpu.PrefetchScalarGridSpec(
        num_scalar_prefetch=6,
        grid=(nt,),
        in_specs=[
            pl.BlockSpec((ROUTE_TILE, D_MODEL), lambda i, *_: (i, 0)),
            pl.BlockSpec((N_EXPERTS, ROUTE_TILE), lambda i, *_: (0, i)),
            full3(thr), full3(need), tile3(), tile3(),
        ],
        out_specs=[
            pl.BlockSpec(memory_space=pl.ANY),
            pl.BlockSpec((N_EXPERTS, ROUTE_TILE), lambda i, *_: (0, i)),
        ],
        scratch_shapes=[
            pltpu.VMEM((N_EXPERTS * ROUTE_TILE, XE_W), BF16),
            pltpu.VMEM((N_EXPERTS * ROUTE_TILE, ROUTE_TILE), BF16),
            pltpu.VMEM((N_EXPERTS, PIECE, XE_W), BF16),
            pltpu.VMEM((PIECE, XE_W), BF16),
            pltpu.SemaphoreType.DMA,
        ],
    )
    return pl.pallas_call(
        functools.partial(_dispatch_kernel, tile_group=tile_group, cap_rows=cap_rows, real_rows=real_rows),
        grid_spec=grid_spec,
        out_shape=[
            jax.ShapeDtypeStruct((N_EXPERTS * cap_rows, XE_W), BF16),
            jax.ShapeDtypeStruct((N_EXPERTS, n), I32),
        ],
        compiler_params=_params(("arbitrary",)),
    )(meta["abase"], meta["npiece"], meta["off"], meta["nch"], meta["nflush"], meta["rem"],
      h2, aff_t, thr, need, meta["ceq_v"], meta["pos_v"])


def _expert_kernel(xe_ref, wg_ref, wu_ref, wd_ref, ye_ref):
    e = pl.program_id(0)
    x = xe_ref[:, :D_MODEL]
    gc = xe_ref[:, D_MODEL:].astype(F32)
    lane = lax.broadcasted_iota(I32, gc.shape, 1)
    gate = jnp.sum(jnp.where((lane & (N_EXPERTS - 1)) == e, gc, 0.0), axis=1, keepdims=True)
    hg = jnp.dot(x, wg_ref[0], preferred_element_type=F32)
    hu = jnp.dot(x, wu_ref[0], preferred_element_type=F32)
    hid = (jax.nn.silu(hg) * hu).astype(BF16)
    y = jnp.dot(hid, wd_ref[0], preferred_element_type=F32) * gate
    ye_ref[...] = y.astype(BF16)


def _experts(xe, wg, wu, wd, l, cap_rows):
    rt = cap_rows // EXPERT_TILE
    rows = lambda w_: pl.BlockSpec((EXPERT_TILE, w_), lambda e, r: (e * rt + r, 0))
    weight = lambda w: pl.BlockSpec((None, 1) + w.shape[2:], lambda e, r: (l, e, 0, 0))
    return pl.pallas_call(
        _expert_kernel,
        grid=(N_EXPERTS, rt),
        in_specs=[rows(XE_W), weight(wg), weight(wu), weight(wd)],
        out_specs=rows(D_MODEL),
        out_shape=jax.ShapeDtypeStruct((N_EXPERTS * cap_rows, D_MODEL), BF16),
        compiler_params=_params(("arbitrary", "arbitrary")),
    )(xe, wg, wu, wd)


def _combine_kernel(abase_s, npiece_s, off_s, nch_s, rall_s, x1_ref, gpos_ref, ye_ref, x2_ref,
                    stage, pbuf, sem, *, cap_rows):
    i = pl.program_id(0)

    @pl.when(i == 0)
    def _():
        stage[...] = jnp.zeros(stage.shape, BF16)

    gpos = gpos_ref[...]

    def copy(e, j, row):
        src = pl.multiple_of(e * cap_rows + abase_s[i * N_EXPERTS + e] + j * PIECE, PIECE)
        return pltpu.make_async_copy(ye_ref.at[pl.ds(src, PIECE)], stage.at[pl.ds(row, PIECE)], sem)

    def fetch(e, j, row):
        copy(e, j, row).start()
        pbuf[pl.ds(row, PIECE), :] = jnp.where(_piece_onehot(gpos, e, row), 1.0, 0.0).astype(BF16)

    _segment_pieces(i, npiece_s, off_s, fetch)

    def clear(j, _):
        row = pl.multiple_of(rall_s[i] + j * PIECE, PIECE)
        pbuf[pl.ds(row, PIECE), :] = jnp.zeros((PIECE, ROUTE_TILE), BF16)
        return 0

    lax.fori_loop(0, (nch_s[i] * MXU_DIM - rall_s[i]) // PIECE, clear, 0)
    x2_ref[...] = x1_ref[...]
    _segment_pieces(i, npiece_s, off_s, lambda e, j, row: copy(e, j, row).wait())

    def chunk(c, _):
        r0 = pl.multiple_of(c * MXU_DIM, MXU_DIM)
        onehot_t = pbuf[pl.ds(r0, MXU_DIM), :].astype(F32).T.astype(BF16)
        x2_ref[...] += jnp.dot(onehot_t, stage[pl.ds(r0, MXU_DIM), :], preferred_element_type=F32)
        return 0

    lax.fori_loop(0, nch_s[i], chunk, 0)


def _combine(meta, x1, gpos, ye, cap_rows):
    n = x1.shape[0]
    grid_spec = pltpu.PrefetchScalarGridSpec(
        num_scalar_prefetch=5,
        grid=(n // ROUTE_TILE,),
        in_specs=[
            pl.BlockSpec((ROUTE_TILE, D_MODEL), lambda i, *_: (i, 0)),
            pl.BlockSpec((N_EXPERTS, ROUTE_TILE), lambda i, *_: (0, i)),
            pl.BlockSpec(memory_space=pl.ANY),
        ],
        out_specs=pl.BlockSpec((ROUTE_TILE, D_MODEL), lambda i, *_: (i, 0)),
        scratch_shapes=[
            pltpu.VMEM((N_EXPERTS * ROUTE_TILE, D_MODEL), BF16),
            pltpu.VMEM((N_EXPERTS * ROUTE_TILE, ROUTE_TILE), BF16),
            pltpu.SemaphoreType.DMA,
        ],
    )
    return pl.pallas_call(
        functools.partial(_combine_kernel, cap_rows=cap_rows),
        grid_spec=grid_spec,
        out_shape=jax.ShapeDtypeStruct((n, D_MODEL), F32),
        compiler_params=_params(("arbitrary",)),
    )(meta["abase"], meta["npiece"], meta["off"], meta["nch"], meta["rall"], x1, gpos, ye)


def _final_kernel(x_ref, g_ref, op_ref, os_ref, *, tiles_p):
    i = pl.program_id(0)
    y = _rms(x_ref[...], g_ref[...])

    @pl.when(i < tiles_p)
    def _():
        op_ref[...] = y

    @pl.when(i >= tiles_p)
    def _():
        os_ref[...] = y


def _final_norm(x, g, n_p):
    n = x.shape[0]
    tiles_p = n_p // ROW_TILE
    return pl.pallas_call(
        functools.partial(_final_kernel, tiles_p=tiles_p),
        grid=(n // ROW_TILE,),
        in_specs=[pl.BlockSpec((ROW_TILE, D_MODEL), lambda i: (i, 0)),
                  pl.BlockSpec((1, D_MODEL), lambda i: (0, 0))],
        out_specs=[
            pl.BlockSpec((ROW_TILE, D_MODEL), lambda i: (jnp.minimum(i, tiles_p - 1), 0)),
            pl.BlockSpec((ROW_TILE, D_MODEL), lambda i: (jnp.maximum(i - tiles_p, 0), 0)),
        ],
        out_shape=[
            jax.ShapeDtypeStruct((n_p, D_MODEL), F32),
            jax.ShapeDtypeStruct((n - n_p, D_MODEL), F32),
        ],
        compiler_params=_params(("arbitrary",)),
    )(x, g)


def _route_meta(gtc, eqc, need, tile_grp, group_first_tile):
    gtc = gtc[:, :, 0].astype(I32)
    eqc = eqc[:, :, 0].astype(I32)
    need_t = need[:, :, 0].astype(I32)[tile_grp]
    ceq = jnp.cumsum(eqc, axis=0) - eqc
    ceq = ceq - ceq[group_first_tile]
    cnt = gtc + jnp.clip(need_t - ceq, 0, eqc)
    first = jnp.cumsum(cnt, axis=0) - cnt
    rem = first % PIECE
    span = jnp.where(cnt > 0, (rem + cnt + PIECE - 1) // PIECE * PIECE, 0)
    off = jnp.cumsum(span, axis=1) - span
    rall = jnp.sum(span, axis=1)
    nch = (rall + MXU_DIM - 1) // MXU_DIM
    nflush = jnp.where(cnt > 0, (rem + cnt) // PIECE, 0)
    bcast = lambda a, dt: jnp.broadcast_to(a[:, :, None].astype(dt), a.shape + (LANES,))
    return {
        "abase": (first - rem).reshape(-1), "npiece": (span // PIECE).reshape(-1), "off": off.reshape(-1),
        "nflush": nflush.reshape(-1), "rem": rem.reshape(-1), "nch": nch, "rall": rall,
        "ceq_v": bcast(ceq, F32), "pos_v": bcast(off + rem, I32),
    }


def kernel(x_prompt, x_sample, rel_bias, norm1_g, w_in, attn_sink, conv_w, conv_b, w_rec_a, b_rec_a,
           w_rec_x, b_rec_x, lru_lambda, grp_g_attn, grp_g_lru, w_out, norm2_g, w_router,
           w_e_gate, w_e_up, w_e_down, final_g):
    bp, seq, _ = x_prompt.shape
    bs = x_sample.shape[0]
    assert x_sample.shape[1] == seq
    nseq = bp + bs
    n_p, n_s = bp * seq, bs * seq
    n = n_p + n_s
    assert seq % ATTN_TQ == 0 and n_p % ROW_TILE == 0 and n_s % ROW_TILE == 0
    assert n_p % ROUTE_TILE == 0 and n_s % ROUTE_TILE == 0
    depth = w_in.shape[0]

    groups = ((0, n_p, EC_CAPACITY * n_p // N_EXPERTS), (n_p, n, EC_CAPACITY * n_s // N_EXPERTS))
    nt = n // ROUTE_TILE
    nt_p = n_p // ROUTE_TILE
    tile_grp = np.where(np.arange(nt) < nt_p, 0, 1)
    group_first_tile = np.where(np.arange(nt) < nt_p, 0, nt_p)
    real_rows = groups[0][2] + groups[1][2]
    assert real_rows % PIECE == 0
    cap_rows = -(-real_rows // EXPERT_TILE) * EXPERT_TILE
    tile_group = lambda i: (i >= nt_p).astype(I32)

    bias_tbl = _bias_table(rel_bias)
    row = lambda v: v.reshape(1, -1).astype(F32)
    x = None
    w_in_b, w_out_b = w_in.astype(BF16), w_out.astype(BF16)
    w_gate_b, w_up_b, w_down_b = w_e_gate.astype(BF16), w_e_up.astype(BF16), w_e_down.astype(BF16)

    for l in range(depth):
        if l == 0:
            qkv, xy, x = _inproj_first(x_prompt.reshape(n_p, D_MODEL), x_sample.reshape(n_s, D_MODEL),
                                       row(norm1_g[l]), w_in_b)
        else:
            qkv, xy = _inproj(x, row(norm1_g[l]), w_in_b, l)
        sink_tbl = jnp.broadcast_to(attn_sink[l].astype(F32)[:, None, None], (N_HEADS, 1, LANES))
        attn = _attention(qkv, bias_tbl, sink_tbl, nseq, seq).reshape(n, ATTN_W)
        w_rec = jnp.concatenate([w_rec_a[l], w_rec_x[l]], axis=-1).astype(BF16)
        lru = _lru(xy, conv_w[l].astype(F32), conv_b[l].astype(F32), w_rec, b_rec_a[l].astype(F32),
                   b_rec_x[l].astype(F32), lru_lambda[l].astype(F32), nseq, seq).reshape(n, LRU_W)
        x1, h2, aff_t = _outproj(attn, lru, x, row(grp_g_attn[l]), row(grp_g_lru[l]),
                                 w_out_b, l, row(norm2_g[l]), _split_router(w_router[l]))
        thr, need, gtc, eqc = _thresholds(aff_t, groups)
        meta = _route_meta(gtc, eqc, need, tile_grp, group_first_tile)
        xe, gpos = _dispatch(meta, h2, aff_t, thr, need, tile_group, cap_rows, real_rows)
        ye = _experts(xe, w_gate_b, w_up_b, w_down_b, l, cap_rows)
        x = _combine(meta, x1, gpos, ye, cap_rows)

    y_p, y_s = _final_norm(x, row(final_g), n_p)
    return (y_p.reshape(bp, seq, D_MODEL), y_s.reshape(bs, seq, D_MODEL))
```

```python
import functools

import numpy as np
import jax
import jax.numpy as jnp
from jax import lax
from jax.experimental import pallas as pl
from jax.experimental.pallas import tpu as pltpu

F32 = jnp.float32
BF16 = jnp.bfloat16
I32 = jnp.int32

D_MODEL = 2048
HEAD_DIM = 128
N_HEADS = 8
N_KV_HEADS = 2
GQA_GROUP = N_HEADS // N_KV_HEADS
ATTN_W = N_HEADS * HEAD_DIM
KV_W = N_KV_HEADS * HEAD_DIM
QKV_W = ATTN_W + 2 * KV_W
WINDOW = 128
BLOCK = 128
N_BUCKETS = 32
MAX_DISTANCE = 128
LRU_W = D_MODEL - ATTN_W
LRU_BLOCKS = 8
LRU_BLK = LRU_W // LRU_BLOCKS
CONV_W = 4
LRU_C = 8.0
N_EXPERTS = 16
EC_CAPACITY = 2
D_EXPERT = D_MODEL // 2
EPS = 1e-6
NEG = -1e30

LANES = 128
SUBLANES = 8
BF16_ROWS = 16
MXU_DIM = 256
VMEM_LIMIT = 56 * 1024 * 1024

ROW_TILE = 512
ATTN_TQ = 512
ATTN_SKEW = 2
LRU_FINISH = 256
LRU_SEG = 60
ROUTE_TILE = 256
PIECE = BF16_ROWS
EXPERT_TILE = 512
GATE_COLS = LANES
XE_W = D_MODEL + GATE_COLS


def _params(sem, **kw):
    return pltpu.CompilerParams(dimension_semantics=sem, vmem_limit_bytes=VMEM_LIMIT, **kw)


def _rms(x, g):
    return x * lax.rsqrt(jnp.mean(x * x, axis=-1, keepdims=True) + EPS) * g


def _inproj_kernel(x_ref, g_ref, w_ref, qkv_ref, xy_ref):
    h = _rms(x_ref[...], g_ref[...]).astype(BF16)
    qkv_ref[...] = jnp.dot(h, w_ref[:, :QKV_W], preferred_element_type=F32).astype(BF16)
    xy_ref[...] = jnp.dot(h, w_ref[:, QKV_W:], preferred_element_type=F32)


def _layer_weight(w, l):
    return pl.BlockSpec((None,) + w.shape[1:], lambda i: (l, 0, 0), pipeline_mode=pl.Buffered(1))


def _inproj(x, g, w, l):
    n = x.shape[0]
    return pl.pallas_call(
        _inproj_kernel,
        grid=(n // ROW_TILE,),
        in_specs=[
            pl.BlockSpec((ROW_TILE, D_MODEL), lambda i: (i, 0)),
            pl.BlockSpec((1, D_MODEL), lambda i: (0, 0)),
            _layer_weight(w, l),
        ],
        out_specs=[
            pl.BlockSpec((ROW_TILE, QKV_W), lambda i: (i, 0)),
            pl.BlockSpec((ROW_TILE, 2 * LRU_W), lambda i: (i, 0)),
        ],
        out_shape=[
            jax.ShapeDtypeStruct((n, QKV_W), BF16),
            jax.ShapeDtypeStruct((n, 2 * LRU_W), F32),
        ],
        compiler_params=_params(("arbitrary",)),
    )(x, g, w)


def _inproj_first_kernel(xp_ref, xs_ref, g_ref, w_ref, qkv_ref, xy_ref, x_ref, *, tiles_p):
    i = pl.program_id(0)

    def run(src):
        x_ref[...] = src[...]
        _inproj_kernel(src, g_ref, w_ref, qkv_ref, xy_ref)

    pl.when(i < tiles_p)(lambda: run(xp_ref))
    pl.when(i >= tiles_p)(lambda: run(xs_ref))


def _inproj_first(xp, xs, g, w):
    n_p, n_s = xp.shape[0], xs.shape[0]
    n = n_p + n_s
    tile = ROW_TILE // 2
    tiles_p = n_p // tile
    row = lambda w_: pl.BlockSpec((tile, w_), lambda i: (i, 0))
    return pl.pallas_call(
        functools.partial(_inproj_first_kernel, tiles_p=tiles_p),
        grid=(n // tile,),
        in_specs=[
            pl.BlockSpec((tile, D_MODEL), lambda i: (jnp.minimum(i, tiles_p - 1), 0)),
            pl.BlockSpec((tile, D_MODEL), lambda i: (jnp.maximum(i - tiles_p, 0), 0)),
            pl.BlockSpec((1, D_MODEL), lambda i: (0, 0)),
            _layer_weight(w, 0),
        ],
        out_specs=[row(QKV_W), row(2 * LRU_W), row(D_MODEL)],
        out_shape=[
            jax.ShapeDtypeStruct((n, QKV_W), BF16),
            jax.ShapeDtypeStruct((n, 2 * LRU_W), F32),
            jax.ShapeDtypeStruct((n, D_MODEL), F32),
        ],
        compiler_params=_params(("arbitrary",)),
    )(xp, xs, g, w)


def _attn_kernel(q_ref, kp_ref, km_ref, kn_ref, vp_ref, vm_ref, vn_ref, bias_ref, sink_ref, o_ref,
                 s_scr, p_scr, m_scr, d_scr, *, seq):
    i = pl.program_id(2)
    kwin = jnp.concatenate([kp_ref[0], km_ref[0], kn_ref[0]], axis=0)
    vwin = jnp.concatenate([vp_ref[0], vm_ref[0], vn_ref[0]], axis=0)
    qi = lax.broadcasted_iota(I32, (BLOCK, 3 * BLOCK), 0)
    ji = lax.broadcasted_iota(I32, (BLOCK, 3 * BLOCK), 1)
    band = jnp.abs(ji - BLOCK - qi) <= WINDOW
    scale = HEAD_DIM ** -0.5
    nsb = ATTN_TQ // BLOCK
    unit = lambda sb, g: sb * GQA_GROUP + g
    qrows = lambda sb: slice(sb * BLOCK, (sb + 1) * BLOCK)
    hcols = lambda g: slice(g * HEAD_DIM, (g + 1) * HEAD_DIM)
    sink = [sink_ref[g][:, :1] for g in range(GQA_GROUP)]
    ones = jnp.ones((3 * BLOCK, HEAD_DIM), BF16)
    valid, vext = [], []
    for sb in range(nsb):
        kpos = i * ATTN_TQ + (sb - 1) * BLOCK + ji
        valid.append(band & (kpos >= 0) & (kpos < seq))
        vext.append(jnp.concatenate([vwin[sb * BLOCK:sb * BLOCK + 3 * BLOCK], ones], axis=1))

    def scores(sb, g):
        s = lax.dot_general(q_ref[0, qrows(sb), hcols(g)], kwin[sb * BLOCK:sb * BLOCK + 3 * BLOCK],
                            (((1,), (1,)), ((), ())), preferred_element_type=F32)
        s_scr[unit(sb, g)] = jnp.where(valid[sb], s * scale + bias_ref[g], NEG)

    def row_max(sb, g):
        u = unit(sb, g)
        m_scr[u] = jnp.maximum(jnp.max(s_scr[u], axis=-1, keepdims=True), sink[g])

    def exponent(sb, g):
        u = unit(sb, g)
        p_scr[u] = jnp.exp(s_scr[u] - m_scr[u]).astype(BF16)

    def values(sb, g):
        u = unit(sb, g)
        o = jnp.dot(p_scr[u], vext[sb], preferred_element_type=F32)
        denom = o[:, HEAD_DIM:HEAD_DIM + 1] + jnp.exp(sink[g] - m_scr[u])
        o_ref[0, qrows(sb), hcols(g)] = o[:, :HEAD_DIM] / denom

    passes = (scores, row_max, exponent, values)
    units = [(sb, g) for sb in range(nsb) for g in range(GQA_GROUP)]
    for t in range(len(units) + ATTN_SKEW * (len(passes) - 1)):
        for k, fn in enumerate(passes):
            if 0 <= t - ATTN_SKEW * k < len(units):
                fn(*units[t - ATTN_SKEW * k])


def _attention(qkv, bias_tbl, sink_tbl, nseq, seq):
    qkv3 = qkv.reshape(nseq, seq, QKV_W)
    per = ATTN_TQ // BLOCK
    last = seq // BLOCK - 1
    kcol = ATTN_W // HEAD_DIM
    vcol = (ATTN_W + KV_W) // HEAD_DIM

    def side(col, prev):
        if prev:
            return pl.BlockSpec((1, BLOCK, HEAD_DIM), lambda b, k, i: (b, jnp.maximum(i * per - 1, 0), col + k))
        return pl.BlockSpec((1, BLOCK, HEAD_DIM), lambda b, k, i: (b, jnp.minimum(i * per + per, last), col + k))

    def main(col):
        return pl.BlockSpec((1, ATTN_TQ, HEAD_DIM), lambda b, k, i: (b, i, col + k))

    qw = GQA_GROUP * HEAD_DIM
    return pl.pallas_call(
        functools.partial(_attn_kernel, seq=seq),
        grid=(nseq, N_KV_HEADS, seq // ATTN_TQ),
        in_specs=[
            pl.BlockSpec((1, ATTN_TQ, qw), lambda b, k, i: (b, i, k)),
            side(kcol, True), main(kcol), side(kcol, False),
            side(vcol, True), main(vcol), side(vcol, False),
            pl.BlockSpec((GQA_GROUP, BLOCK, 3 * BLOCK), lambda b, k, i: (k, 0, 0)),
            pl.BlockSpec((GQA_GROUP, 1, LANES), lambda b, k, i: (k, 0, 0)),
        ],
        out_specs=pl.BlockSpec((1, ATTN_TQ, qw), lambda b, k, i: (b, i, k)),
        out_shape=jax.ShapeDtypeStruct((nseq, seq, ATTN_W), F32),
        scratch_shapes=[
            pltpu.VMEM((per * GQA_GROUP, BLOCK, 3 * BLOCK), F32),
            pltpu.VMEM((per * GQA_GROUP, BLOCK, 3 * BLOCK), BF16),
            pltpu.VMEM((per * GQA_GROUP, BLOCK, 1), F32),
            pltpu.VMEM((per * GQA_GROUP, BLOCK, 1), F32),
        ],
        compiler_params=_params(("arbitrary", "arbitrary", "arbitrary")),
    )(qkv3, qkv3, qkv3, qkv3, qkv3, qkv3, qkv3, bias_tbl, sink_tbl)


def _t5_bucket(rel):
    nb = N_BUCKETS // 2
    ret = (rel > 0).astype(np.int32) * nb
    n = np.abs(rel)
    max_exact = nb // 2
    large = max_exact + (np.log(np.maximum(n, 1) / max_exact) / np.log(MAX_DISTANCE / max_exact)
                         * (nb - max_exact)).astype(np.int32)
    large = np.minimum(large, nb - 1)
    return (ret + np.where(n < max_exact, n, large)).astype(np.int32)


def _bias_table(rel_bias):
    rel = (np.arange(3 * BLOCK)[None, :] - BLOCK) - np.arange(BLOCK)[:, None]
    onehot = np.eye(N_BUCKETS, dtype=np.float32)[_t5_bucket(rel)]
    return jnp.einsum("qkb,bh->hqk", onehot, rel_bias.astype(F32), precision=lax.Precision.HIGHEST)


def _sublane_scan(a, u, reverse):
    sub = lax.broadcasted_iota(I32, a.shape, 0)
    for d in (1, 2, 4):
        keep = (sub < SUBLANES - d) if reverse else (sub >= d)
        shift = SUBLANES - d if reverse else d
        a_sh = jnp.where(keep, pltpu.roll(a, shift, 0), 1.0)
        u_sh = jnp.where(keep, pltpu.roll(u, shift, 0), 0.0)
        u = a * u_sh + u
        a = a * a_sh
    return a, u


def _lru_chunk(xr_ref, base, seg, d, prm, state):
    cw, cb, w, ba, bx, decay = prm
    halo, carry = state[:3], state[3]
    sub = lax.broadcasted_iota(I32, (SUBLANES, LRU_BLK), 0)
    x = [xr_ref[0, pl.ds(base + j, SUBLANES, stride=seg), :] for j in range(seg)]
    if d == 0:
        edge = [jnp.where(sub == 0, halo[m], pltpu.roll(x[seg - 1 - m], 1, 0)) for m in range(3)]
        at = lambda j: x[j] if j >= 0 else edge[-j - 1]
        taps = lambda j: [at(j - k) for k in range(CONV_W)]
        new_halo = [x[seg - 1 - m][SUBLANES - 1:SUBLANES] for m in range(3)]
    else:
        edge = [jnp.where(sub == SUBLANES - 1, halo[m], pltpu.roll(x[m], SUBLANES - 1, 0)) for m in range(3)]
        at = lambda j: x[j] if j < seg else edge[j - seg]
        taps = lambda j: [at(j + k) for k in range(CONV_W)]
        new_halo = [x[m][0:1] for m in range(3)]
    xc = []
    for j in range(seg):
        t = taps(j)
        acc = cb[d] + cw[d][CONV_W - 1] * t[0]
        for k in range(1, CONV_W):
            acc = acc + cw[d][CONV_W - 1 - k] * t[k]
        xc.append(acc)
    xc_all = jnp.concatenate(xc, axis=0)
    z = jnp.dot(xc_all.astype(BF16), w[d], preferred_element_type=F32)
    r = jax.nn.sigmoid(z[:, :LRU_BLK] + ba[d])
    ig = jax.nn.sigmoid(z[:, LRU_BLK:] + bx[d])
    log_a = decay[d] * r
    a_all = jnp.exp(log_a)
    th = jnp.tanh(log_a)
    u_all = jnp.sqrt(-2.0 * th / (1.0 - th)) * (ig * xc_all)
    a = [a_all[j * SUBLANES:(j + 1) * SUBLANES] for j in range(seg)]
    u = [u_all[j * SUBLANES:(j + 1) * SUBLANES] for j in range(seg)]
    order = list(range(seg)) if d == 0 else list(range(seg - 1, -1, -1))
    hl, pr = [None] * seg, [None] * seg
    prev = None
    for j in order:
        if prev is None:
            hl[j], pr[j] = u[j], a[j]
        else:
            hl[j] = a[j] * hl[prev] + u[j]
            pr[j] = a[j] * pr[prev]
        prev = j
    ai, ui = _sublane_scan(pr[prev], hl[prev], d == 1)
    end = ui + ai * carry
    if d == 0:
        start = jnp.where(sub == 0, carry, pltpu.roll(end, 1, 0))
        new_carry = end[SUBLANES - 1:SUBLANES]
    else:
        start = jnp.where(sub == SUBLANES - 1, carry, pltpu.roll(end, SUBLANES - 1, 0))
        new_carry = end[0:1]
    h = [hl[j] + pr[j] * start for j in range(seg)]
    return h, tuple(new_halo) + (new_carry,)


def _lru_kernel(xr_ref, yr_ref, cw_ref, cb_ref, w_ref, ba_ref, bx_ref, lam_ref, o_ref, hb_ref, *, seq):
    chunk = SUBLANES * LRU_SEG
    nfull = seq // chunk
    tail_seg = (seq - nfull * chunk) // SUBLANES
    prm = (
        [[cw_ref[d, j:j + 1, :] for j in range(CONV_W)] for d in range(2)],
        [cb_ref[d] for d in range(2)],
        [w_ref[d, 0] for d in range(2)],
        [ba_ref[d] for d in range(2)],
        [bx_ref[d] for d in range(2)],
        [-LRU_C * jax.nn.softplus(-lam_ref[d]) for d in range(2)],
    )
    rows = lambda base, j, seg: (0, pl.ds(base + j, SUBLANES, stride=seg), slice(None))

    def scan(dst_ref, base, seg, d, state):
        h, state = _lru_chunk(xr_ref, base, seg, d, prm, state)
        for j in range(seg):
            dst_ref[rows(base, j, seg)] = h[j]
        return state

    def both(k, st):
        fwd = scan(o_ref, k * chunk, LRU_SEG, 0, st[0])
        bwd = scan(hb_ref, (nfull - 1 - k) * chunk, LRU_SEG, 1, st[1])
        return (fwd, bwd)

    zero = jnp.zeros((1, LRU_BLK), F32)
    bwd = (zero,) * 4
    if tail_seg:
        bwd = scan(hb_ref, nfull * chunk, tail_seg, 1, bwd)
    fwd, _ = lax.fori_loop(0, nfull, both, ((zero,) * 4, bwd))
    if tail_seg:
        scan(o_ref, nfull * chunk, tail_seg, 0, fwd)

    def finish(k, _):
        blk = (0, pl.ds(pl.multiple_of(k * LRU_FINISH, LRU_FINISH), LRU_FINISH), slice(None))
        o_ref[blk] = (o_ref[blk] + hb_ref[blk]) * jax.nn.gelu(yr_ref[blk])
        return 0

    lax.fori_loop(0, seq // LRU_FINISH, finish, 0)


def _lru(xy, cw, cb, w, ba, bx, lam, nseq, seq):
    tail = seq % (SUBLANES * LRU_SEG)
    assert tail % SUBLANES == 0 and (tail == 0 or tail // SUBLANES >= CONV_W - 1) and seq % LRU_FINISH == 0
    xy3 = xy.reshape(nseq, seq, 2 * LRU_W)
    vec = lambda: pl.BlockSpec((2, 1, LRU_BLK), lambda b, c: (0, 0, c))
    return pl.pallas_call(
        functools.partial(_lru_kernel, seq=seq),
        grid=(nseq, LRU_BLOCKS),
        in_specs=[
            pl.BlockSpec((1, seq, LRU_BLK), lambda b, c: (b, 0, c)),
            pl.BlockSpec((1, seq, LRU_BLK), lambda b, c: (b, 0, LRU_BLOCKS + c)),
            pl.BlockSpec((2, CONV_W, LRU_BLK), lambda b, c: (0, 0, c)),
            vec(),
            pl.BlockSpec((2, 1, LRU_BLK, 2 * LRU_BLK), lambda b, c: (0, c, 0, 0)),
            vec(), vec(), vec(),
        ],
        out_specs=pl.BlockSpec((1, seq, LRU_BLK), lambda b, c: (b, 0, c)),
        out_shape=jax.ShapeDtypeStruct((nseq, seq, LRU_W), F32),
        scratch_shapes=[pltpu.VMEM((1, seq, LRU_BLK), F32)],
        compiler_params=_params(("arbitrary", "arbitrary")),
    )(xy3, xy3, cw, cb.reshape(2, 1, LRU_W), w, ba.reshape(2, 1, LRU_W),
      bx.reshape(2, 1, LRU_W), lam.reshape(2, 1, LRU_W))


def _outproj_kernel(attn_ref, lru_ref, x_ref, ga_ref, gl_ref, w_ref, g2_ref, wr_ref, x1_ref, h2_ref, aff_ref):
    an = _rms(attn_ref[...], ga_ref[...]).astype(BF16)
    ln = _rms(lru_ref[...], gl_ref[...]).astype(BF16)
    y = jnp.dot(an, w_ref[:ATTN_W], preferred_element_type=F32)
    y = y + jnp.dot(ln, w_ref[ATTN_W:], preferred_element_type=F32)
    x1 = x_ref[...] + y
    x1_ref[...] = x1
    hf = _rms(x1, g2_ref[...])
    h_hi = hf.astype(BF16)
    h2_ref[...] = h_hi
    h_lo = (hf - h_hi.astype(F32)).astype(BF16)
    nt = (((1,), (1,)), ((), ()))
    a = lax.dot_general(wr_ref[...], h_hi, nt, preferred_element_type=F32)
    b = lax.dot_general(wr_ref[:N_EXPERTS], h_lo, nt, preferred_element_type=F32)
    logits = a[:N_EXPERTS] + a[N_EXPERTS:] + b
    e = jnp.exp(logits - jnp.max(logits, axis=0, keepdims=True))
    aff_ref[...] = e / jnp.sum(e, axis=0, keepdims=True)


def _outproj(attn, lru, x, ga, gl, w, l, g2, wr_t):
    n = x.shape[0]
    row = lambda w_: pl.BlockSpec((ROW_TILE, w_), lambda i: (i, 0))
    full = lambda a: pl.BlockSpec(a.shape, lambda i: (0,) * a.ndim)
    return pl.pallas_call(
        _outproj_kernel,
        grid=(n // ROW_TILE,),
        in_specs=[row(ATTN_W), row(LRU_W), row(D_MODEL), full(ga), full(gl),
                  _layer_weight(w, l), full(g2), full(wr_t)],
        out_specs=[row(D_MODEL), row(D_MODEL), pl.BlockSpec((N_EXPERTS, ROW_TILE), lambda i: (0, i))],
        out_shape=[
            jax.ShapeDtypeStruct((n, D_MODEL), F32),
            jax.ShapeDtypeStruct((n, D_MODEL), BF16),
            jax.ShapeDtypeStruct((N_EXPERTS, n), F32),
        ],
        compiler_params=_params(("arbitrary",)),
    )(attn, lru, x, ga, gl, w, g2, wr_t)


def _split_router(w_router):
    wt = w_router.astype(F32).T
    hi = wt.astype(BF16)
    lo = (wt - hi.astype(F32)).astype(BF16)
    return jnp.concatenate([hi, lo], axis=0)


def _threshold_kernel(aff_ref, thr_ref, need_ref, gt_ref, eq_ref, *, groups):
    for gi, (lo, hi, cap) in enumerate(groups):
        def keys():
            return lax.bitcast_convert_type(aff_ref[:, lo:hi], I32)

        def step(it, thr):
            cand = thr | jnp.left_shift(jnp.int32(1), 30 - it)
            cnt = jnp.sum((keys() >= cand).astype(F32), axis=1, keepdims=True)
            return jnp.where(cnt >= cap, cand, thr)

        thr = lax.fori_loop(0, 31, step, jnp.zeros((N_EXPERTS, 1), I32))
        ngt = jnp.sum((keys() > thr).astype(F32), axis=1, keepdims=True)
        thr_ref[gi] = jnp.broadcast_to(thr, (N_EXPERTS, LANES))
        need_ref[gi] = jnp.broadcast_to(cap - ngt, (N_EXPERTS, LANES))
        for t in range(lo // ROUTE_TILE, hi // ROUTE_TILE):
            kt = lax.bitcast_convert_type(aff_ref[:, t * ROUTE_TILE:(t + 1) * ROUTE_TILE], I32)
            gt_ref[t] = jnp.broadcast_to(jnp.sum((kt > thr).astype(F32), axis=1, keepdims=True), (N_EXPERTS, LANES))
            eq_ref[t] = jnp.broadcast_to(jnp.sum((kt == thr).astype(F32), axis=1, keepdims=True), (N_EXPERTS, LANES))


def _thresholds(aff_t, groups):
    n = aff_t.shape[1]
    nt = n // ROUTE_TILE
    ng = len(groups)
    return pl.pallas_call(
        functools.partial(_threshold_kernel, groups=groups),
        out_shape=[
            jax.ShapeDtypeStruct((ng, N_EXPERTS, LANES), I32),
            jax.ShapeDtypeStruct((ng, N_EXPERTS, LANES), F32),
            jax.ShapeDtypeStruct((nt, N_EXPERTS, LANES), F32),
            jax.ShapeDtypeStruct((nt, N_EXPERTS, LANES), F32),
        ],
        compiler_params=pltpu.CompilerParams(vmem_limit_bytes=VMEM_LIMIT),
    )(aff_t)


def _segment_pieces(tile, npiece_s, off_s, fn):
    for e in range(N_EXPERTS):
        npiece = npiece_s[tile * N_EXPERTS + e]
        first = off_s[tile * N_EXPERTS + e]

        def body(j, _, e=e, first=first):
            fn(e, j, pl.multiple_of(first + j * PIECE, PIECE))
            return 0

        lax.fori_loop(0, npiece, body, 0)


def _wait_pieces(piece_copy, count):
    def body(j, _):
        piece_copy.wait()
        return 0

    lax.fori_loop(0, count, body, 0)


def _piece_onehot(gpos, e, row):
    rid = lax.broadcasted_iota(I32, (PIECE, gpos.shape[1]), 0) + row
    return gpos[e:e + 1, :] == rid


def _dispatch_kernel(abase_s, npiece_s, off_s, nch_s, nflush_s, rem_s,
                     h2_ref, aff_ref, thr_ref, need_ref, ceq_ref, offv_ref,
                     xe_ref, gpos_ref, stage, pbuf, carry, zeros, sem, *, tile_group, cap_rows, real_rows):
    i = pl.program_id(0)
    tm = ROUTE_TILE

    @pl.when(i == 0)
    def _():
        pbuf[...] = jnp.zeros(pbuf.shape, BF16)

    grp = tile_group(i)
    thr = jnp.where(grp == 0, thr_ref[0], thr_ref[1])[:, :1]
    need = jnp.where(grp == 0, need_ref[0], need_ref[1])[:, :1]
    aff = aff_ref[...]
    keys = lax.bitcast_convert_type(aff, I32)
    gt = keys > thr
    eq = keys == thr
    before = (lax.broadcasted_iota(I32, (tm, tm), 0) < lax.broadcasted_iota(I32, (tm, tm), 1))
    before = jnp.where(before, 1.0, 0.0).astype(BF16)
    eq_rank = jnp.dot(jnp.where(eq, 1.0, 0.0).astype(BF16), before, preferred_element_type=F32)
    sel = gt | (eq & (ceq_ref[0][:, :1] + eq_rank < need))
    rank = jnp.dot(jnp.where(sel, 1.0, 0.0).astype(BF16), before, preferred_element_type=F32)
    gpos = jnp.where(sel, offv_ref[0][:, :1] + rank.astype(I32), -1)
    gpos_ref[...] = gpos

    g_hi = aff.astype(BF16).astype(F32)
    g_mid = (aff - g_hi).astype(BF16).astype(F32)
    g_lo = (aff - g_hi - g_mid).astype(BF16).astype(F32)
    pad = jnp.zeros((GATE_COLS - 3 * N_EXPERTS, tm), F32)
    gate_cols = jnp.concatenate([g_hi, g_mid, g_lo, pad], axis=0).T.astype(BF16)

    def build(e, j, row):
        pbuf[pl.ds(row, PIECE), :] = jnp.where(_piece_onehot(gpos, e, row), 1.0, 0.0).astype(BF16)

    _segment_pieces(i, npiece_s, off_s, build)

    def chunk(c, _):
        r0 = pl.multiple_of(c * MXU_DIM, MXU_DIM)
        onehot = pbuf[pl.ds(r0, MXU_DIM), :]
        rows = jnp.dot(onehot, h2_ref[...], preferred_element_type=F32)
        stage[pl.ds(r0, MXU_DIM), pl.ds(0, D_MODEL)] = rows.astype(BF16)
        gates = jnp.dot(onehot, gate_cols, preferred_element_type=F32)
        stage[pl.ds(r0, MXU_DIM), pl.ds(D_MODEL, GATE_COLS)] = gates.astype(BF16)
        return 0

    lax.fori_loop(0, nch_s[i], chunk, 0)

    for e in range(N_EXPERTS):
        k = i * N_EXPERTS + e

        @pl.when((npiece_s[k] > 0) & (rem_s[k] > 0))
        def _(e=e, k=k):
            head = pl.ds(pl.multiple_of(off_s[k], PIECE), PIECE)
            stage[head, :] = stage[head, :] + carry[e]

    def copy(e, j, row):
        dst = pl.multiple_of(e * cap_rows + abase_s[i * N_EXPERTS + e] + j * PIECE, PIECE)
        return pltpu.make_async_copy(stage.at[pl.ds(row, PIECE)], xe_ref.at[pl.ds(dst, PIECE)], sem)

    _segment_pieces(i, nflush_s, off_s, lambda e, j, row: copy(e, j, row).start())

    for e in range(N_EXPERTS):
        k = i * N_EXPERTS + e

        @pl.when(npiece_s[k] > nflush_s[k])
        def _(e=e, k=k):
            carry[e] = stage[pl.ds(pl.multiple_of(off_s[k] + nflush_s[k] * PIECE, PIECE), PIECE), :]

    _wait_pieces(copy(0, 0, 0), nch_s[pl.num_programs(0) + i])

    ntail = (cap_rows - real_rows) // PIECE
    if ntail:
        @pl.when(i == pl.num_programs(0) - 1)
        def _():
            zeros[...] = jnp.zeros(zeros.shape, BF16)
            fills = [pltpu.make_async_copy(zeros, xe_ref.at[pl.ds(e * cap_rows + real_rows + j * PIECE, PIECE)], sem)
                     for e in range(N_EXPERTS) for j in range(ntail)]
            for f in fills:
                f.start()
            for f in fills:
                f.wait()


def _dispatch(meta, h2, aff_t, thr, need, tile_group, cap_rows, real_rows):
    n = h2.shape[0]
    nt = n // ROUTE_TILE
    tile3 = lambda: pl.BlockSpec((1, N_EXPERTS, LANES), lambda i, *_: (i, 0, 0))
    full3 = lambda a: pl.BlockSpec(a.shape, lambda i, *_: (0, 0, 0))
    grid_spec = pltpu.PrefetchScalarGridSpec(
        num_scalar_prefetch=6,
        grid=(nt,),
        in_specs=[
            pl.BlockSpec((ROUTE_TILE, D_MODEL), lambda i, *_: (i, 0)),
            pl.BlockSpec((N_EXPERTS, ROUTE_TILE), lambda i, *_: (0, i)),
            full3(thr), full3(need), tile3(), tile3(),
        ],
        out_specs=[
            pl.BlockSpec(memory_space=pl.ANY),
            pl.BlockSpec((N_EXPERTS, ROUTE_TILE), lambda i, *_: (0, i)),
        ],
        scratch_shapes=[
            pltpu.VMEM((N_EXPERTS * ROUTE_TILE, XE_W), BF16),
            pltpu.VMEM((N_EXPERTS * ROUTE_TILE, ROUTE_TILE), BF16),
            pltpu.VMEM((N_EXPERTS, PIECE, XE_W), BF16),
            pltpu.VMEM((PIECE, XE_W), BF16),
            pltpu.SemaphoreType.DMA,
        ],
    )
    return pl.pallas_call(
        functools.partial(_dispatch_kernel, tile_group=tile_group, cap_rows=cap_rows, real_rows=real_rows),
        grid_spec=grid_spec,
        out_shape=[
            jax.ShapeDtypeStruct((N_EXPERTS * cap_rows, XE_W), BF16),
            jax.ShapeDtypeStruct((N_EXPERTS, n), I32),
        ],
        compiler_params=_params(("arbitrary",)),
    )(meta["abase"], meta["npiece"], meta["off"], meta["nch_nflush"], meta["nflush"], meta["rem"],
      h2, aff_t, thr, need, meta["ceq_v"], meta["pos_v"])


def _expert_kernel(xe_ref, wg_ref, wu_ref, wd_ref, ye_ref):
    e = pl.program_id(0)
    x = xe_ref[:, :D_MODEL]
    gc = xe_ref[:, D_MODEL:].astype(F32)
    lane = lax.broadcasted_iota(I32, gc.shape, 1)
    gate = jnp.sum(jnp.where((lane & (N_EXPERTS - 1)) == e, gc, 0.0), axis=1, keepdims=True)
    hg = jnp.dot(x, wg_ref[0], preferred_element_type=F32)
    hu = jnp.dot(x, wu_ref[0], preferred_element_type=F32)
    hid = (jax.nn.silu(hg) * hu).astype(BF16)
    y = jnp.dot(hid, wd_ref[0], preferred_element_type=F32) * gate
    ye_ref[...] = y.astype(BF16)


def _experts(xe, wg, wu, wd, l, cap_rows):
    rt = cap_rows // EXPERT_TILE
    rows = lambda w_: pl.BlockSpec((EXPERT_TILE, w_), lambda e, r: (e * rt + r, 0))
    weight = lambda w: pl.BlockSpec((None, 1) + w.shape[2:], lambda e, r: (l, e, 0, 0))
    return pl.pallas_call(
        _expert_kernel,
        grid=(N_EXPERTS, rt),
        in_specs=[rows(XE_W), weight(wg), weight(wu), weight(wd)],
        out_specs=rows(D_MODEL),
        out_shape=jax.ShapeDtypeStruct((N_EXPERTS * cap_rows, D_MODEL), BF16),
        compiler_params=_params(("arbitrary", "arbitrary")),
    )(xe, wg, wu, wd)


def _combine_kernel(abase_s, npiece_s, off_s, nch_s, rall_s, x1_ref, gpos_ref, ye_ref, x2_ref,
                    stage, pbuf, sem, *, cap_rows):
    i = pl.program_id(0)

    @pl.when(i == 0)
    def _():
        stage[...] = jnp.zeros(stage.shape, BF16)

    gpos = gpos_ref[...]

    def copy(e, j, row):
        src = pl.multiple_of(e * cap_rows + abase_s[i * N_EXPERTS + e] + j * PIECE, PIECE)
        return pltpu.make_async_copy(ye_ref.at[pl.ds(src, PIECE)], stage.at[pl.ds(row, PIECE)], sem)

    def fetch(e, j, row):
        copy(e, j, row).start()
        pbuf[pl.ds(row, PIECE), :] = jnp.where(_piece_onehot(gpos, e, row), 1.0, 0.0).astype(BF16)

    _segment_pieces(i, npiece_s, off_s, fetch)

    def clear(j, _):
        row = pl.multiple_of(rall_s[i] + j * PIECE, PIECE)
        pbuf[pl.ds(row, PIECE), :] = jnp.zeros((PIECE, ROUTE_TILE), BF16)
        return 0

    lax.fori_loop(0, (nch_s[i] * MXU_DIM - rall_s[i]) // PIECE, clear, 0)
    _wait_pieces(copy(0, 0, 0), rall_s[i] // PIECE)

    def chunk(c):
        r0 = pl.multiple_of(c * MXU_DIM, MXU_DIM)
        onehot_t = pbuf[pl.ds(r0, MXU_DIM), :].astype(F32).T.astype(BF16)
        return jnp.dot(onehot_t, stage[pl.ds(r0, MXU_DIM), :], preferred_element_type=F32)

    @pl.when(nch_s[i] == 0)
    def _():
        x2_ref[...] = x1_ref[...]

    @pl.when(nch_s[i] > 0)
    def _():
        x2_ref[...] = x1_ref[...] + chunk(0)

    def more(c, _):
        x2_ref[...] += chunk(c)
        return 0

    lax.fori_loop(1, nch_s[i], more, 0)


def _combine(meta, x1, gpos, ye, cap_rows):
    n = x1.shape[0]
    grid_spec = pltpu.PrefetchScalarGridSpec(
        num_scalar_prefetch=5,
        grid=(n // ROUTE_TILE,),
        in_specs=[
            pl.BlockSpec((ROUTE_TILE, D_MODEL), lambda i, *_: (i, 0)),
            pl.BlockSpec((N_EXPERTS, ROUTE_TILE), lambda i, *_: (0, i)),
            pl.BlockSpec(memory_space=pl.ANY),
        ],
        out_specs=pl.BlockSpec((ROUTE_TILE, D_MODEL), lambda i, *_: (i, 0)),
        scratch_shapes=[
            pltpu.VMEM((N_EXPERTS * ROUTE_TILE, D_MODEL), BF16),
            pltpu.VMEM((N_EXPERTS * ROUTE_TILE, ROUTE_TILE), BF16),
            pltpu.SemaphoreType.DMA,
        ],
    )
    return pl.pallas_call(
        functools.partial(_combine_kernel, cap_rows=cap_rows),
        grid_spec=grid_spec,
        out_shape=jax.ShapeDtypeStruct((n, D_MODEL), F32),
        compiler_params=_params(("arbitrary",)),
    )(meta["abase"], meta["npiece"], meta["off"], meta["nch"], meta["rall"], x1, gpos, ye)


def _final_kernel(x_ref, g_ref, op_ref, os_ref, *, tiles_p):
    i = pl.program_id(0)
    y = _rms(x_ref[...], g_ref[...])

    @pl.when(i < tiles_p)
    def _():
        op_ref[...] = y

    @pl.when(i >= tiles_p)
    def _():
        os_ref[...] = y


def _final_norm(x, g, n_p):
    n = x.shape[0]
    tiles_p = n_p // ROW_TILE
    return pl.pallas_call(
        functools.partial(_final_kernel, tiles_p=tiles_p),
        grid=(n // ROW_TILE,),
        in_specs=[pl.BlockSpec((ROW_TILE, D_MODEL), lambda i: (i, 0)),
                  pl.BlockSpec((1, D_MODEL), lambda i: (0, 0))],
        out_specs=[
            pl.BlockSpec((ROW_TILE, D_MODEL), lambda i: (jnp.minimum(i, tiles_p - 1), 0)),
            pl.BlockSpec((ROW_TILE, D_MODEL), lambda i: (jnp.maximum(i - tiles_p, 0), 0)),
        ],
        out_shape=[
            jax.ShapeDtypeStruct((n_p, D_MODEL), F32),
            jax.ShapeDtypeStruct((n - n_p, D_MODEL), F32),
        ],
        compiler_params=_params(("arbitrary",)),
    )(x, g)


def _route_meta(gtc, eqc, need, tile_grp, group_first_tile):
    gtc = gtc[:, :, 0].astype(I32)
    eqc = eqc[:, :, 0].astype(I32)
    need_t = need[:, :, 0].astype(I32)[tile_grp]
    ceq = jnp.cumsum(eqc, axis=0) - eqc
    ceq = ceq - ceq[group_first_tile]
    cnt = gtc + jnp.clip(need_t - ceq, 0, eqc)
    first = jnp.cumsum(cnt, axis=0) - cnt
    rem = first % PIECE
    span = jnp.where(cnt > 0, (rem + cnt + PIECE - 1) // PIECE * PIECE, 0)
    off = jnp.cumsum(span, axis=1) - span
    rall = jnp.sum(span, axis=1)
    nch = (rall + MXU_DIM - 1) // MXU_DIM
    nflush = jnp.where(cnt > 0, (rem + cnt) // PIECE, 0)
    bcast = lambda a, dt: jnp.broadcast_to(a[:, :, None].astype(dt), a.shape + (LANES,))
    return {
        "abase": (first - rem).reshape(-1), "npiece": (span // PIECE).reshape(-1), "off": off.reshape(-1),
        "nflush": nflush.reshape(-1), "rem": rem.reshape(-1), "nch": nch, "rall": rall,
        "nch_nflush": jnp.concatenate([nch, jnp.sum(nflush, axis=1)]),
        "ceq_v": bcast(ceq, F32), "pos_v": bcast(off + rem, I32),
    }


def kernel(x_prompt, x_sample, rel_bias, norm1_g, w_in, attn_sink, conv_w, conv_b, w_rec_a, b_rec_a,
           w_rec_x, b_rec_x, lru_lambda, grp_g_attn, grp_g_lru, w_out, norm2_g, w_router,
           w_e_gate, w_e_up, w_e_down, final_g):
    bp, seq, _ = x_prompt.shape
    bs = x_sample.shape[0]
    assert x_sample.shape[1] == seq
    nseq = bp + bs
    n_p, n_s = bp * seq, bs * seq
    n = n_p + n_s
    assert seq % ATTN_TQ == 0 and n_p % ROW_TILE == 0 and n_s % ROW_TILE == 0
    assert n_p % ROUTE_TILE == 0 and n_s % ROUTE_TILE == 0
    depth = w_in.shape[0]

    groups = ((0, n_p, EC_CAPACITY * n_p // N_EXPERTS), (n_p, n, EC_CAPACITY * n_s // N_EXPERTS))
    nt = n // ROUTE_TILE
    nt_p = n_p // ROUTE_TILE
    tile_grp = np.where(np.arange(nt) < nt_p, 0, 1)
    group_first_tile = np.where(np.arange(nt) < nt_p, 0, nt_p)
    real_rows = groups[0][2] + groups[1][2]
    assert real_rows % PIECE == 0
    cap_rows = -(-real_rows // EXPERT_TILE) * EXPERT_TILE
    tile_group = lambda i: (i >= nt_p).astype(I32)

    bias_tbl = _bias_table(rel_bias)
    row = lambda v: v.reshape(1, -1).astype(F32)
    x = None
    w_in_b, w_out_b = w_in.astype(BF16), w_out.astype(BF16)
    w_gate_b, w_up_b, w_down_b = w_e_gate.astype(BF16), w_e_up.astype(BF16), w_e_down.astype(BF16)

    for l in range(depth):
        if l == 0:
            qkv, xy, x = _inproj_first(x_prompt.reshape(n_p, D_MODEL), x_sample.reshape(n_s, D_MODEL),
                                       row(norm1_g[l]), w_in_b)
        else:
            qkv, xy = _inproj(x, row(norm1_g[l]), w_in_b, l)
        sink_tbl = jnp.broadcast_to(attn_sink[l].astype(F32)[:, None, None], (N_HEADS, 1, LANES))
        attn = _attention(qkv, bias_tbl, sink_tbl, nseq, seq).reshape(n, ATTN_W)
        w_rec = jnp.concatenate([w_rec_a[l], w_rec_x[l]], axis=-1).astype(BF16)
        lru = _lru(xy, conv_w[l].astype(F32), conv_b[l].astype(F32), w_rec, b_rec_a[l].astype(F32),
                   b_rec_x[l].astype(F32), lru_lambda[l].astype(F32), nseq, seq).reshape(n, LRU_W)
        x1, h2, aff_t = _outproj(attn, lru, x, row(grp_g_attn[l]), row(grp_g_lru[l]),
                                 w_out_b, l, row(norm2_g[l]), _split_router(w_router[l]))
        thr, need, gtc, eqc = _thresholds(aff_t, groups)
        meta = _route_meta(gtc, eqc, need, tile_grp, group_first_tile)
        xe, gpos = _dispatch(meta, h2, aff_t, thr, need, tile_group, cap_rows, real_rows)
        ye = _experts(xe, w_gate_b, w_up_b, w_down_b, l, cap_rows)
        x = _combine(meta, x1, gpos, ye, cap_rows)

    y_p, y_s = _final_norm(x, row(final_g), n_p)
    return (y_p.reshape(bp, seq, D_MODEL), y_s.reshape(bs, seq, D_MODEL))
```

```python
import functools

import numpy as np
import jax
import jax.numpy as jnp
from jax import lax
from jax.experimental import pallas as pl
from jax.experimental.pallas import tpu as pltpu

F32 = jnp.float32
BF16 = jnp.bfloat16
I32 = jnp.int32

D_MODEL = 2048
HEAD_DIM = 128
N_HEADS = 8
N_KV_HEADS = 2
GQA_GROUP = N_HEADS // N_KV_HEADS
ATTN_W = N_HEADS * HEAD_DIM
KV_W = N_KV_HEADS * HEAD_DIM
QKV_W = ATTN_W + 2 * KV_W
WINDOW = 128
BLOCK = 128
N_BUCKETS = 32
MAX_DISTANCE = 128
LRU_W = D_MODEL - ATTN_W
LRU_BLOCKS = 8
LRU_BLK = LRU_W // LRU_BLOCKS
CONV_W = 4
LRU_C = 8.0
N_EXPERTS = 16
EC_CAPACITY = 2
D_EXPERT = D_MODEL // 2
EPS = 1e-6
NEG = -1e30

LANES = 128
SUBLANES = 8
BF16_ROWS = 16
MXU_DIM = 256
VMEM_LIMIT = 56 * 1024 * 1024

ROW_TILE = 512
ATTN_TQ = 512
ATTN_SKEW = 2
LRU_FINISH = 256
LRU_SEG = 60
ROUTE_TILE = 256
PIECE = BF16_ROWS
EXPERT_TILE = 512
GATE_COLS = LANES
XE_W = D_MODEL + GATE_COLS


def _params(sem, **kw):
    return pltpu.CompilerParams(dimension_semantics=sem, vmem_limit_bytes=VMEM_LIMIT, **kw)


def _rms(x, g):
    return x * lax.rsqrt(jnp.mean(x * x, axis=-1, keepdims=True) + EPS) * g


def _inproj_kernel(x_ref, g_ref, w_ref, qkv_ref, xy_ref):
    h = _rms(x_ref[...], g_ref[...]).astype(BF16)
    qkv_ref[...] = jnp.dot(h, w_ref[:, :QKV_W], preferred_element_type=F32).astype(BF16)
    xy_ref[...] = jnp.dot(h, w_ref[:, QKV_W:], preferred_element_type=F32)


def _layer_weight(w, l):
    return pl.BlockSpec((None,) + w.shape[1:], lambda i: (l, 0, 0), pipeline_mode=pl.Buffered(1))


def _inproj(x, g, w, l):
    n = x.shape[0]
    return pl.pallas_call(
        _inproj_kernel,
        grid=(n // ROW_TILE,),
        in_specs=[
            pl.BlockSpec((ROW_TILE, D_MODEL), lambda i: (i, 0)),
            pl.BlockSpec((1, D_MODEL), lambda i: (0, 0)),
            _layer_weight(w, l),
        ],
        out_specs=[
            pl.BlockSpec((ROW_TILE, QKV_W), lambda i: (i, 0)),
            pl.BlockSpec((ROW_TILE, 2 * LRU_W), lambda i: (i, 0)),
        ],
        out_shape=[
            jax.ShapeDtypeStruct((n, QKV_W), BF16),
            jax.ShapeDtypeStruct((n, 2 * LRU_W), F32),
        ],
        compiler_params=_params(("arbitrary",)),
    )(x, g, w)


def _inproj_first_kernel(xp_ref, xs_ref, g_ref, w_ref, qkv_ref, xy_ref, x_ref, *, tiles_p):
    i = pl.program_id(0)

    def run(src):
        x_ref[...] = src[...]
        _inproj_kernel(src, g_ref, w_ref, qkv_ref, xy_ref)

    pl.when(i < tiles_p)(lambda: run(xp_ref))
    pl.when(i >= tiles_p)(lambda: run(xs_ref))


def _inproj_first(xp, xs, g, w):
    n_p, n_s = xp.shape[0], xs.shape[0]
    n = n_p + n_s
    tile = ROW_TILE // 2
    tiles_p = n_p // tile
    row = lambda w_: pl.BlockSpec((tile, w_), lambda i: (i, 0))
    return pl.pallas_call(
        functools.partial(_inproj_first_kernel, tiles_p=tiles_p),
        grid=(n // tile,),
        in_specs=[
            pl.BlockSpec((tile, D_MODEL), lambda i: (jnp.minimum(i, tiles_p - 1), 0)),
            pl.BlockSpec((tile, D_MODEL), lambda i: (jnp.maximum(i - tiles_p, 0), 0)),
            pl.BlockSpec((1, D_MODEL), lambda i: (0, 0)),
            _layer_weight(w, 0),
        ],
        out_specs=[row(QKV_W), row(2 * LRU_W), row(D_MODEL)],
        out_shape=[
            jax.ShapeDtypeStruct((n, QKV_W), BF16),
            jax.ShapeDtypeStruct((n, 2 * LRU_W), F32),
            jax.ShapeDtypeStruct((n, D_MODEL), F32),
        ],
        compiler_params=_params(("arbitrary",)),
    )(xp, xs, g, w)


def _attn_kernel(q_ref, kp_ref, km_ref, kn_ref, vp_ref, vm_ref, vn_ref, bias_ref, sink_ref, o_ref,
                 s_scr, p_scr, m_scr, d_scr, *, seq):
    i = pl.program_id(2)
    kwin = jnp.concatenate([kp_ref[0], km_ref[0], kn_ref[0]], axis=0)
    vwin = jnp.concatenate([vp_ref[0], vm_ref[0], vn_ref[0]], axis=0)
    qi = lax.broadcasted_iota(I32, (BLOCK, 3 * BLOCK), 0)
    ji = lax.broadcasted_iota(I32, (BLOCK, 3 * BLOCK), 1)
    band = jnp.abs(ji - BLOCK - qi) <= WINDOW
    scale = HEAD_DIM ** -0.5
    nsb = ATTN_TQ // BLOCK
    unit = lambda sb, g: sb * GQA_GROUP + g
    qrows = lambda sb: slice(sb * BLOCK, (sb + 1) * BLOCK)
    hcols = lambda g: slice(g * HEAD_DIM, (g + 1) * HEAD_DIM)
    sink = [sink_ref[g][:, :1] for g in range(GQA_GROUP)]
    ones = jnp.ones((3 * BLOCK, HEAD_DIM), BF16)
    valid, vext = [], []
    for sb in range(nsb):
        kpos = i * ATTN_TQ + (sb - 1) * BLOCK + ji
        valid.append(band & (kpos >= 0) & (kpos < seq))
        vext.append(jnp.concatenate([vwin[sb * BLOCK:sb * BLOCK + 3 * BLOCK], ones], axis=1))

    def scores(sb, g):
        s = lax.dot_general(q_ref[0, qrows(sb), hcols(g)], kwin[sb * BLOCK:sb * BLOCK + 3 * BLOCK],
                            (((1,), (1,)), ((), ())), preferred_element_type=F32)
        s_scr[unit(sb, g)] = jnp.where(valid[sb], s * scale + bias_ref[g], NEG)

    def row_max(sb, g):
        u = unit(sb, g)
        m_scr[u] = jnp.maximum(jnp.max(s_scr[u], axis=-1, keepdims=True), sink[g])

    def exponent(sb, g):
        u = unit(sb, g)
        p_scr[u] = jnp.exp(s_scr[u] - m_scr[u]).astype(BF16)

    def values(sb, g):
        u = unit(sb, g)
        o = jnp.dot(p_scr[u], vext[sb], preferred_element_type=F32)
        denom = o[:, HEAD_DIM:HEAD_DIM + 1] + jnp.exp(sink[g] - m_scr[u])
        o_ref[0, qrows(sb), hcols(g)] = o[:, :HEAD_DIM] / denom

    passes = (scores, row_max, exponent, values)
    units = [(sb, g) for sb in range(nsb) for g in range(GQA_GROUP)]
    for t in range(len(units) + ATTN_SKEW * (len(passes) - 1)):
        for k, fn in enumerate(passes):
            if 0 <= t - ATTN_SKEW * k < len(units):
                fn(*units[t - ATTN_SKEW * k])


def _attention(qkv, bias_tbl, sink_tbl, nseq, seq):
    qkv3 = qkv.reshape(nseq, seq, QKV_W)
    per = ATTN_TQ // BLOCK
    last = seq // BLOCK - 1
    kcol = ATTN_W // HEAD_DIM
    vcol = (ATTN_W + KV_W) // HEAD_DIM

    def side(col, prev):
        if prev:
            return pl.BlockSpec((1, BLOCK, HEAD_DIM), lambda b, k, i: (b, jnp.maximum(i * per - 1, 0), col + k))
        return pl.BlockSpec((1, BLOCK, HEAD_DIM), lambda b, k, i: (b, jnp.minimum(i * per + per, last), col + k))

    def main(col):
        return pl.BlockSpec((1, ATTN_TQ, HEAD_DIM), lambda b, k, i: (b, i, col + k))

    qw = GQA_GROUP * HEAD_DIM
    return pl.pallas_call(
        functools.partial(_attn_kernel, seq=seq),
        grid=(nseq, N_KV_HEADS, seq // ATTN_TQ),
        in_specs=[
            pl.BlockSpec((1, ATTN_TQ, qw), lambda b, k, i: (b, i, k)),
            side(kcol, True), main(kcol), side(kcol, False),
            side(vcol, True), main(vcol), side(vcol, False),
            pl.BlockSpec((GQA_GROUP, BLOCK, 3 * BLOCK), lambda b, k, i: (k, 0, 0)),
            pl.BlockSpec((GQA_GROUP, 1, LANES), lambda b, k, i: (k, 0, 0)),
        ],
        out_specs=pl.BlockSpec((1, ATTN_TQ, qw), lambda b, k, i: (b, i, k)),
        out_shape=jax.ShapeDtypeStruct((nseq, seq, ATTN_W), F32),
        scratch_shapes=[
            pltpu.VMEM((per * GQA_GROUP, BLOCK, 3 * BLOCK), F32),
            pltpu.VMEM((per * GQA_GROUP, BLOCK, 3 * BLOCK), BF16),
            pltpu.VMEM((per * GQA_GROUP, BLOCK, 1), F32),
            pltpu.VMEM((per * GQA_GROUP, BLOCK, 1), F32),
        ],
        compiler_params=_params(("arbitrary", "arbitrary", "arbitrary")),
    )(qkv3, qkv3, qkv3, qkv3, qkv3, qkv3, qkv3, bias_tbl, sink_tbl)


def _t5_bucket(rel):
    nb = N_BUCKETS // 2
    ret = (rel > 0).astype(np.int32) * nb
    n = np.abs(rel)
    max_exact = nb // 2
    large = max_exact + (np.log(np.maximum(n, 1) / max_exact) / np.log(MAX_DISTANCE / max_exact)
                         * (nb - max_exact)).astype(np.int32)
    large = np.minimum(large, nb - 1)
    return (ret + np.where(n < max_exact, n, large)).astype(np.int32)


def _bias_table(rel_bias):
    rel = (np.arange(3 * BLOCK)[None, :] - BLOCK) - np.arange(BLOCK)[:, None]
    onehot = np.eye(N_BUCKETS, dtype=np.float32)[_t5_bucket(rel)]
    return jnp.einsum("qkb,bh->hqk", onehot, rel_bias.astype(F32), precision=lax.Precision.HIGHEST)


def _sublane_scan(a, u, reverse):
    sub = lax.broadcasted_iota(I32, a.shape, 0)
    for d in (1, 2, 4):
        keep = (sub < SUBLANES - d) if reverse else (sub >= d)
        shift = SUBLANES - d if reverse else d
        a_sh = jnp.where(keep, pltpu.roll(a, shift, 0), 1.0)
        u_sh = jnp.where(keep, pltpu.roll(u, shift, 0), 0.0)
        u = a * u_sh + u
        a = a * a_sh
    return a, u


def _lru_chunk(xr_ref, base, seg, d, prm, state):
    cw, cb, w, ba, bx, decay = prm
    halo, carry = state[:3], state[3]
    sub = lax.broadcasted_iota(I32, (SUBLANES, LRU_BLK), 0)
    x = [xr_ref[0, pl.ds(base + j, SUBLANES, stride=seg), :] for j in range(seg)]
    if d == 0:
        edge = [jnp.where(sub == 0, halo[m], pltpu.roll(x[seg - 1 - m], 1, 0)) for m in range(3)]
        at = lambda j: x[j] if j >= 0 else edge[-j - 1]
        taps = lambda j: [at(j - k) for k in range(CONV_W)]
        new_halo = [x[seg - 1 - m][SUBLANES - 1:SUBLANES] for m in range(3)]
    else:
        edge = [jnp.where(sub == SUBLANES - 1, halo[m], pltpu.roll(x[m], SUBLANES - 1, 0)) for m in range(3)]
        at = lambda j: x[j] if j < seg else edge[j - seg]
        taps = lambda j: [at(j + k) for k in range(CONV_W)]
        new_halo = [x[m][0:1] for m in range(3)]
    xc = []
    for j in range(seg):
        t = taps(j)
        acc = cb[d] + cw[d][CONV_W - 1] * t[0]
        for k in range(1, CONV_W):
            acc = acc + cw[d][CONV_W - 1 - k] * t[k]
        xc.append(acc)
    xc_all = jnp.concatenate(xc, axis=0)
    z = jnp.dot(xc_all.astype(BF16), w[d], preferred_element_type=F32)
    r = jax.nn.sigmoid(z[:, :LRU_BLK] + ba[d])
    ig = jax.nn.sigmoid(z[:, LRU_BLK:] + bx[d])
    log_a = decay[d] * r
    a_all = jnp.exp(log_a)
    th = jnp.tanh(log_a)
    u_all = jnp.sqrt(-2.0 * th / (1.0 - th)) * (ig * xc_all)
    a = [a_all[j * SUBLANES:(j + 1) * SUBLANES] for j in range(seg)]
    u = [u_all[j * SUBLANES:(j + 1) * SUBLANES] for j in range(seg)]
    order = list(range(seg)) if d == 0 else list(range(seg - 1, -1, -1))
    hl, pr = [None] * seg, [None] * seg
    prev = None
    for j in order:
        if prev is None:
            hl[j], pr[j] = u[j], a[j]
        else:
            hl[j] = a[j] * hl[prev] + u[j]
            pr[j] = a[j] * pr[prev]
        prev = j
    ai, ui = _sublane_scan(pr[prev], hl[prev], d == 1)
    end = ui + ai * carry
    if d == 0:
        start = jnp.where(sub == 0, carry, pltpu.roll(end, 1, 0))
        new_carry = end[SUBLANES - 1:SUBLANES]
    else:
        start = jnp.where(sub == SUBLANES - 1, carry, pltpu.roll(end, SUBLANES - 1, 0))
        new_carry = end[0:1]
    h = [hl[j] + pr[j] * start for j in range(seg)]
    return h, tuple(new_halo) + (new_carry,)


def _lru_kernel(xr_ref, yr_ref, cw_ref, cb_ref, w_ref, ba_ref, bx_ref, lam_ref, o_ref, hb_ref, *, seq):
    chunk = SUBLANES * LRU_SEG
    nfull = seq // chunk
    tail_seg = (seq - nfull * chunk) // SUBLANES
    prm = (
        [[cw_ref[d, j:j + 1, :] for j in range(CONV_W)] for d in range(2)],
        [cb_ref[d] for d in range(2)],
        [w_ref[d, 0] for d in range(2)],
        [ba_ref[d] for d in range(2)],
        [bx_ref[d] for d in range(2)],
        [-LRU_C * jax.nn.softplus(-lam_ref[d]) for d in range(2)],
    )
    rows = lambda base, j, seg: (0, pl.ds(base + j, SUBLANES, stride=seg), slice(None))

    def scan(dst_ref, base, seg, d, state):
        h, state = _lru_chunk(xr_ref, base, seg, d, prm, state)
        for j in range(seg):
            dst_ref[rows(base, j, seg)] = h[j]
        return state

    def both(k, st):
        fwd = scan(o_ref, k * chunk, LRU_SEG, 0, st[0])
        bwd = scan(hb_ref, (nfull - 1 - k) * chunk, LRU_SEG, 1, st[1])
        return (fwd, bwd)

    zero = jnp.zeros((1, LRU_BLK), F32)
    bwd = (zero,) * 4
    if tail_seg:
        bwd = scan(hb_ref, nfull * chunk, tail_seg, 1, bwd)
    fwd, _ = lax.fori_loop(0, nfull, both, ((zero,) * 4, bwd))
    if tail_seg:
        scan(o_ref, nfull * chunk, tail_seg, 0, fwd)

    def finish(k, _):
        blk = (0, pl.ds(pl.multiple_of(k * LRU_FINISH, LRU_FINISH), LRU_FINISH), slice(None))
        o_ref[blk] = (o_ref[blk] + hb_ref[blk]) * jax.nn.gelu(yr_ref[blk])
        return 0

    lax.fori_loop(0, seq // LRU_FINISH, finish, 0)


def _lru(xy, cw, cb, w, ba, bx, lam, nseq, seq):
    tail = seq % (SUBLANES * LRU_SEG)
    assert tail % SUBLANES == 0 and (tail == 0 or tail // SUBLANES >= CONV_W - 1) and seq % LRU_FINISH == 0
    xy3 = xy.reshape(nseq, seq, 2 * LRU_W)
    vec = lambda: pl.BlockSpec((2, 1, LRU_BLK), lambda b, c: (0, 0, c))
    return pl.pallas_call(
        functools.partial(_lru_kernel, seq=seq),
        grid=(nseq, LRU_BLOCKS),
        in_specs=[
            pl.BlockSpec((1, seq, LRU_BLK), lambda b, c: (b, 0, c)),
            pl.BlockSpec((1, seq, LRU_BLK), lambda b, c: (b, 0, LRU_BLOCKS + c)),
            pl.BlockSpec((2, CONV_W, LRU_BLK), lambda b, c: (0, 0, c)),
            vec(),
            pl.BlockSpec((2, 1, LRU_BLK, 2 * LRU_BLK), lambda b, c: (0, c, 0, 0)),
            vec(), vec(), vec(),
        ],
        out_specs=pl.BlockSpec((1, seq, LRU_BLK), lambda b, c: (b, 0, c)),
        out_shape=jax.ShapeDtypeStruct((nseq, seq, LRU_W), F32),
        scratch_shapes=[pltpu.VMEM((1, seq, LRU_BLK), F32)],
        compiler_params=_params(("arbitrary", "arbitrary")),
    )(xy3, xy3, cw, cb.reshape(2, 1, LRU_W), w, ba.reshape(2, 1, LRU_W),
      bx.reshape(2, 1, LRU_W), lam.reshape(2, 1, LRU_W))


def _outproj_kernel(attn_ref, lru_ref, x_ref, ga_ref, gl_ref, w_ref, g2_ref, wr_ref, x1_ref, h2_ref, aff_ref):
    an = _rms(attn_ref[...], ga_ref[...]).astype(BF16)
    ln = _rms(lru_ref[...], gl_ref[...]).astype(BF16)
    y = jnp.dot(an, w_ref[:ATTN_W], preferred_element_type=F32)
    y = y + jnp.dot(ln, w_ref[ATTN_W:], preferred_element_type=F32)
    x1 = x_ref[...] + y
    x1_ref[...] = x1
    hf = _rms(x1, g2_ref[...])
    h_hi = hf.astype(BF16)
    h2_ref[...] = h_hi
    h_lo = (hf - h_hi.astype(F32)).astype(BF16)
    nt = (((1,), (1,)), ((), ()))
    a = lax.dot_general(wr_ref[...], h_hi, nt, preferred_element_type=F32)
    b = lax.dot_general(wr_ref[:N_EXPERTS], h_lo, nt, preferred_element_type=F32)
    logits = a[:N_EXPERTS] + a[N_EXPERTS:] + b
    e = jnp.exp(logits - jnp.max(logits, axis=0, keepdims=True))
    aff_ref[...] = e / jnp.sum(e, axis=0, keepdims=True)


def _outproj(attn, lru, x, ga, gl, w, l, g2, wr_t):
    n = x.shape[0]
    row = lambda w_: pl.BlockSpec((ROW_TILE, w_), lambda i: (i, 0))
    full = lambda a: pl.BlockSpec(a.shape, lambda i: (0,) * a.ndim)
    return pl.pallas_call(
        _outproj_kernel,
        grid=(n // ROW_TILE,),
        in_specs=[row(ATTN_W), row(LRU_W), row(D_MODEL), full(ga), full(gl),
                  _layer_weight(w, l), full(g2), full(wr_t)],
        out_specs=[row(D_MODEL), row(D_MODEL), pl.BlockSpec((N_EXPERTS, ROW_TILE), lambda i: (0, i))],
        out_shape=[
            jax.ShapeDtypeStruct((n, D_MODEL), F32),
            jax.ShapeDtypeStruct((n, D_MODEL), BF16),
            jax.ShapeDtypeStruct((N_EXPERTS, n), F32),
        ],
        compiler_params=_params(("arbitrary",)),
    )(attn, lru, x, ga, gl, w, g2, wr_t)


def _split_router(w_router):
    wt = w_router.astype(F32).T
    hi = wt.astype(BF16)
    lo = (wt - hi.astype(F32)).astype(BF16)
    return jnp.concatenate([hi, lo], axis=0)


def _threshold_kernel(aff_ref, thr_ref, need_ref, gt_ref, eq_ref, *, groups):
    for gi, (lo, hi, cap) in enumerate(groups):
        def keys():
            return lax.bitcast_convert_type(aff_ref[:, lo:hi], I32)

        def step(it, thr):
            cand = thr | jnp.left_shift(jnp.int32(1), 30 - it)
            cnt = jnp.sum((keys() >= cand).astype(F32), axis=1, keepdims=True)
            return jnp.where(cnt >= cap, cand, thr)

        thr = lax.fori_loop(0, 31, step, jnp.zeros((N_EXPERTS, 1), I32))
        ngt = jnp.sum((keys() > thr).astype(F32), axis=1, keepdims=True)
        thr_ref[gi] = jnp.broadcast_to(thr, (N_EXPERTS, LANES))
        need_ref[gi] = jnp.broadcast_to(cap - ngt, (N_EXPERTS, LANES))
        for t in range(lo // ROUTE_TILE, hi // ROUTE_TILE):
            kt = lax.bitcast_convert_type(aff_ref[:, t * ROUTE_TILE:(t + 1) * ROUTE_TILE], I32)
            gt_ref[t] = jnp.broadcast_to(jnp.sum((kt > thr).astype(F32), axis=1, keepdims=True), (N_EXPERTS, LANES))
            eq_ref[t] = jnp.broadcast_to(jnp.sum((kt == thr).astype(F32), axis=1, keepdims=True), (N_EXPERTS, LANES))


def _thresholds(aff_t, groups):
    n = aff_t.shape[1]
    nt = n // ROUTE_TILE
    ng = len(groups)
    return pl.pallas_call(
        functools.partial(_threshold_kernel, groups=groups),
        out_shape=[
            jax.ShapeDtypeStruct((ng, N_EXPERTS, LANES), I32),
            jax.ShapeDtypeStruct((ng, N_EXPERTS, LANES), F32),
            jax.ShapeDtypeStruct((nt, N_EXPERTS, LANES), F32),
            jax.ShapeDtypeStruct((nt, N_EXPERTS, LANES), F32),
        ],
        compiler_params=pltpu.CompilerParams(vmem_limit_bytes=VMEM_LIMIT),
    )(aff_t)


def _segment_pieces(tile, npiece_s, off_s, fn):
    for e in range(N_EXPERTS):
        npiece = npiece_s[tile * N_EXPERTS + e]
        first = off_s[tile * N_EXPERTS + e]

        def body(j, _, e=e, first=first):
            fn(e, j, pl.multiple_of(first + j * PIECE, PIECE))
            return 0

        lax.fori_loop(0, npiece, body, 0)


def _wait_pieces(piece_copy, count):
    def body(j, _):
        piece_copy.wait()
        return 0

    lax.fori_loop(0, count, body, 0)


def _piece_onehot(gpos, e, row):
    rid = lax.broadcasted_iota(I32, (PIECE, gpos.shape[1]), 0) + row
    return gpos[e:e + 1, :] == rid


def _dispatch_kernel(abase_s, npiece_s, off_s, nch_s, nflush_s, rem_s,
                     h2_ref, aff_ref, thr_ref, need_ref, ceq_ref, offv_ref,
                     xe_ref, gpos_ref, stage, pbuf, carry, zeros, tri, sem, *, tile_group, cap_rows, real_rows):
    i = pl.program_id(0)
    tm = ROUTE_TILE

    @pl.when(i == 0)
    def _():
        pbuf[...] = jnp.zeros(pbuf.shape, BF16)
        earlier = lax.broadcasted_iota(I32, (tm, tm), 0) < lax.broadcasted_iota(I32, (tm, tm), 1)
        tri[...] = jnp.where(earlier, 1.0, 0.0).astype(BF16)

    grp = tile_group(i)
    thr = jnp.where(grp == 0, thr_ref[0], thr_ref[1])[:, :1]
    need = jnp.where(grp == 0, need_ref[0], need_ref[1])[:, :1]
    aff = aff_ref[...]
    keys = lax.bitcast_convert_type(aff, I32)
    gt = keys > thr
    eq = keys == thr
    before = tri[...]
    eq_rank = jnp.dot(jnp.where(eq, 1.0, 0.0).astype(BF16), before, preferred_element_type=F32)
    sel = gt | (eq & (ceq_ref[0][:, :1] + eq_rank < need))
    rank = jnp.dot(jnp.where(sel, 1.0, 0.0).astype(BF16), before, preferred_element_type=F32)
    gpos = jnp.where(sel, offv_ref[0][:, :1] + rank.astype(I32), -1)
    gpos_ref[...] = gpos

    g_hi = aff.astype(BF16).astype(F32)
    g_mid = (aff - g_hi).astype(BF16).astype(F32)
    g_lo = (aff - g_hi - g_mid).astype(BF16).astype(F32)
    pad = jnp.zeros((GATE_COLS - 3 * N_EXPERTS, tm), F32)
    gate_cols = jnp.concatenate([g_hi, g_mid, g_lo, pad], axis=0).T.astype(BF16)

    def build(e, j, row):
        pbuf[pl.ds(row, PIECE), :] = jnp.where(_piece_onehot(gpos, e, row), 1.0, 0.0).astype(BF16)

    _segment_pieces(i, npiece_s, off_s, build)

    nt = pl.num_programs(0)
    slot = lax.rem(i, 2)
    mine = stage.at[slot]

    def chunk(c, _):
        r0 = pl.multiple_of(c * MXU_DIM, MXU_DIM)
        onehot = pbuf[pl.ds(r0, MXU_DIM), :]
        rows = jnp.dot(onehot, h2_ref[...], preferred_element_type=F32)
        mine[pl.ds(r0, MXU_DIM), pl.ds(0, D_MODEL)] = rows.astype(BF16)
        gates = jnp.dot(onehot, gate_cols, preferred_element_type=F32)
        mine[pl.ds(r0, MXU_DIM), pl.ds(D_MODEL, GATE_COLS)] = gates.astype(BF16)
        return 0

    lax.fori_loop(0, nch_s[i], chunk, 0)

    for e in range(N_EXPERTS):
        k = i * N_EXPERTS + e

        @pl.when((npiece_s[k] > 0) & (rem_s[k] > 0))
        def _(e=e, k=k):
            head = pl.ds(pl.multiple_of(off_s[k], PIECE), PIECE)
            mine[head, :] = mine[head, :] + carry[e]

    def copy(s, e, j, row):
        dst = pl.multiple_of(e * cap_rows + abase_s[i * N_EXPERTS + e] + j * PIECE, PIECE)
        return pltpu.make_async_copy(stage.at[s, pl.ds(row, PIECE)], xe_ref.at[pl.ds(dst, PIECE)], sem.at[s])

    _segment_pieces(i, nflush_s, off_s, lambda e, j, row: copy(slot, e, j, row).start())

    for e in range(N_EXPERTS):
        k = i * N_EXPERTS + e

        @pl.when(npiece_s[k] > nflush_s[k])
        def _(e=e, k=k):
            carry[e] = mine[pl.ds(pl.multiple_of(off_s[k] + nflush_s[k] * PIECE, PIECE), PIECE), :]

    @pl.when(i > 0)
    def _():
        _wait_pieces(copy(1 - slot, 0, 0, 0), nch_s[nt + i - 1])

    @pl.when(i == nt - 1)
    def _():
        _wait_pieces(copy(slot, 0, 0, 0), nch_s[nt + i])

    ntail = (cap_rows - real_rows) // PIECE
    if ntail:
        @pl.when(i == nt - 1)
        def _():
            zeros[...] = jnp.zeros(zeros.shape, BF16)
            fills = [pltpu.make_async_copy(zeros, xe_ref.at[pl.ds(e * cap_rows + real_rows + j * PIECE, PIECE)],
                                           sem.at[0])
                     for e in range(N_EXPERTS) for j in range(ntail)]
            for f in fills:
                f.start()
            for f in fills:
                f.wait()


def _dispatch(meta, h2, aff_t, thr, need, tile_group, cap_rows, real_rows):
    n = h2.shape[0]
    nt = n // ROUTE_TILE
    tile3 = lambda: pl.BlockSpec((1, N_EXPERTS, LANES), lambda i, *_: (i, 0, 0))
    full3 = lambda a: pl.BlockSpec(a.shape, lambda i, *_: (0, 0, 0))
    grid_spec = pltpu.PrefetchScalarGridSpec(
        num_scalar_prefetch=6,
        grid=(nt,),
        in_specs=[
            pl.BlockSpec((ROUTE_TILE, D_MODEL), lambda i, *_: (i, 0)),
            pl.BlockSpec((N_EXPERTS, ROUTE_TILE), lambda i, *_: (0, i)),
            full3(thr), full3(need), tile3(), tile3(),
        ],
        out_specs=[
            pl.BlockSpec(memory_space=pl.ANY),
            pl.BlockSpec((N_EXPERTS, ROUTE_TILE), lambda i, *_: (0, i)),
        ],
        scratch_shapes=[
            pltpu.VMEM((2, N_EXPERTS * ROUTE_TILE, XE_W), BF16),
            pltpu.VMEM((N_EXPERTS * ROUTE_TILE, ROUTE_TILE), BF16),
            pltpu.VMEM((N_EXPERTS, PIECE, XE_W), BF16),
            pltpu.VMEM((PIECE, XE_W), BF16),
            pltpu.VMEM((ROUTE_TILE, ROUTE_TILE), BF16),
            pltpu.SemaphoreType.DMA((2,)),
        ],
    )
    return pl.pallas_call(
        functools.partial(_dispatch_kernel, tile_group=tile_group, cap_rows=cap_rows, real_rows=real_rows),
        grid_spec=grid_spec,
        out_shape=[
            jax.ShapeDtypeStruct((N_EXPERTS * cap_rows, XE_W), BF16),
            jax.ShapeDtypeStruct((N_EXPERTS, n), I32),
        ],
        compiler_params=_params(("arbitrary",)),
    )(meta["abase"], meta["npiece"], meta["off"], meta["nch_nflush"], meta["nflush"], meta["rem"],
      h2, aff_t, thr, need, meta["ceq_v"], meta["pos_v"])


def _expert_kernel(xe_ref, wg_ref, wu_ref, wd_ref, ye_ref):
    e = pl.program_id(0)
    x = xe_ref[:, :D_MODEL]
    gc = xe_ref[:, D_MODEL:].astype(F32)
    lane = lax.broadcasted_iota(I32, gc.shape, 1)
    gate = jnp.sum(jnp.where((lane & (N_EXPERTS - 1)) == e, gc, 0.0), axis=1, keepdims=True)
    hg = jnp.dot(x, wg_ref[0], preferred_element_type=F32)
    hu = jnp.dot(x, wu_ref[0], preferred_element_type=F32)
    hid = (jax.nn.silu(hg) * hu).astype(BF16)
    y = jnp.dot(hid, wd_ref[0], preferred_element_type=F32) * gate
    ye_ref[...] = y.astype(BF16)


def _experts(xe, wg, wu, wd, l, cap_rows):
    rt = cap_rows // EXPERT_TILE
    rows = lambda w_: pl.BlockSpec((EXPERT_TILE, w_), lambda e, r: (e * rt + r, 0))
    weight = lambda w: pl.BlockSpec((None, 1) + w.shape[2:], lambda e, r: (l, e, 0, 0))
    return pl.pallas_call(
        _expert_kernel,
        grid=(N_EXPERTS, rt),
        in_specs=[rows(XE_W), weight(wg), weight(wu), weight(wd)],
        out_specs=rows(D_MODEL),
        out_shape=jax.ShapeDtypeStruct((N_EXPERTS * cap_rows, D_MODEL), BF16),
        compiler_params=_params(("arbitrary", "arbitrary")),
    )(xe, wg, wu, wd)


def _combine_kernel(abase_s, npiece_s, off_s, nch_s, rall_s, x1_ref, gpos_ref, ye_ref, x2_ref,
                    stage, pbuf, sem, *, cap_rows):
    i = pl.program_id(0)
    nt = pl.num_programs(0)
    slot = lax.rem(i, 2)

    def copy(t, s, e, j, row):
        src = pl.multiple_of(e * cap_rows + abase_s[t * N_EXPERTS + e] + j * PIECE, PIECE)
        return pltpu.make_async_copy(ye_ref.at[pl.ds(src, PIECE)], stage.at[s, pl.ds(row, PIECE)], sem.at[s])

    def fetch(t, s):
        _segment_pieces(t, npiece_s, off_s, lambda e, j, row: copy(t, s, e, j, row).start())

    @pl.when(i == 0)
    def _():
        stage[...] = jnp.zeros(stage.shape, BF16)
        fetch(0, 0)

    @pl.when(i + 1 < nt)
    def _():
        fetch(i + 1, 1 - slot)

    gpos = gpos_ref[...]

    def build(e, j, row):
        pbuf[pl.ds(row, PIECE), :] = jnp.where(_piece_onehot(gpos, e, row), 1.0, 0.0).astype(BF16)

    _segment_pieces(i, npiece_s, off_s, build)

    def clear(j, _):
        row = pl.multiple_of(rall_s[i] + j * PIECE, PIECE)
        pbuf[pl.ds(row, PIECE), :] = jnp.zeros((PIECE, ROUTE_TILE), BF16)
        return 0

    lax.fori_loop(0, (nch_s[i] * MXU_DIM - rall_s[i]) // PIECE, clear, 0)
    _wait_pieces(copy(i, slot, 0, 0, 0), rall_s[i] // PIECE)
    mine = stage.at[slot]

    def chunk(c):
        r0 = pl.multiple_of(c * MXU_DIM, MXU_DIM)
        onehot_t = pbuf[pl.ds(r0, MXU_DIM), :].astype(F32).T.astype(BF16)
        return jnp.dot(onehot_t, mine[pl.ds(r0, MXU_DIM), :], preferred_element_type=F32)

    @pl.when(nch_s[i] == 0)
    def _():
        x2_ref[...] = x1_ref[...]

    @pl.when(nch_s[i] > 0)
    def _():
        x2_ref[...] = x1_ref[...] + chunk(0)

    def more(c, _):
        x2_ref[...] += chunk(c)
        return 0

    lax.fori_loop(1, nch_s[i], more, 0)


def _combine_final_kernel(abase_s, npiece_s, off_s, nch_s, rall_s, x1_ref, gpos_ref, ye_ref, g_ref,
                          yp_ref, ys_ref, stage, pbuf, sem, acc, *, cap_rows, tiles_p):
    _combine_kernel(abase_s, npiece_s, off_s, nch_s, rall_s, x1_ref, gpos_ref, ye_ref, acc,
                    stage, pbuf, sem, cap_rows=cap_rows)
    i = pl.program_id(0)
    y = _rms(acc[...], g_ref[...])

    @pl.when(i < tiles_p)
    def _():
        yp_ref[...] = y

    @pl.when(i >= tiles_p)
    def _():
        ys_ref[...] = y


def _combine(meta, x1, gpos, ye, cap_rows, final=None):
    n = x1.shape[0]
    tile = lambda: pl.BlockSpec((ROUTE_TILE, D_MODEL), lambda i, *_: (i, 0))
    in_specs = [tile(), pl.BlockSpec((N_EXPERTS, ROUTE_TILE), lambda i, *_: (0, i)),
                pl.BlockSpec(memory_space=pl.ANY)]
    scratch = [
        pltpu.VMEM((2, N_EXPERTS * ROUTE_TILE, D_MODEL), BF16),
        pltpu.VMEM((N_EXPERTS * ROUTE_TILE, ROUTE_TILE), BF16),
        pltpu.SemaphoreType.DMA((2,)),
    ]
    args = (meta["abase"], meta["npiece"], meta["off"], meta["nch"], meta["rall"], x1, gpos, ye)
    if final is None:
        body = functools.partial(_combine_kernel, cap_rows=cap_rows)
        out_specs = tile()
        out_shape = jax.ShapeDtypeStruct((n, D_MODEL), F32)
    else:
        g, n_p = final
        tiles_p = n_p // ROUTE_TILE
        body = functools.partial(_combine_final_kernel, cap_rows=cap_rows, tiles_p=tiles_p)
        in_specs.append(pl.BlockSpec((1, D_MODEL), lambda i, *_: (0, 0)))
        scratch.append(pltpu.VMEM((ROUTE_TILE, D_MODEL), F32))
        out_specs = [
            pl.BlockSpec((ROUTE_TILE, D_MODEL), lambda i, *_: (jnp.minimum(i, tiles_p - 1), 0)),
            pl.BlockSpec((ROUTE_TILE, D_MODEL), lambda i, *_: (jnp.maximum(i - tiles_p, 0), 0)),
        ]
        out_shape = [jax.ShapeDtypeStruct((n_p, D_MODEL), F32), jax.ShapeDtypeStruct((n - n_p, D_MODEL), F32)]
        args = args + (g,)
    grid_spec = pltpu.PrefetchScalarGridSpec(
        num_scalar_prefetch=5, grid=(n // ROUTE_TILE,), in_specs=in_specs, out_specs=out_specs,
        scratch_shapes=scratch)
    return pl.pallas_call(body, grid_spec=grid_spec, out_shape=out_shape,
                          compiler_params=_params(("arbitrary",)))(*args)


def _route_meta(gtc, eqc, need, tile_grp, group_first_tile):
    gtc = gtc[:, :, 0].astype(I32)
    eqc = eqc[:, :, 0].astype(I32)
    need_t = need[:, :, 0].astype(I32)[tile_grp]
    ceq = jnp.cumsum(eqc, axis=0) - eqc
    ceq = ceq - ceq[group_first_tile]
    cnt = gtc + jnp.clip(need_t - ceq, 0, eqc)
    first = jnp.cumsum(cnt, axis=0) - cnt
    rem = first % PIECE
    span = jnp.where(cnt > 0, (rem + cnt + PIECE - 1) // PIECE * PIECE, 0)
    off = jnp.cumsum(span, axis=1) - span
    rall = jnp.sum(span, axis=1)
    nch = (rall + MXU_DIM - 1) // MXU_DIM
    nflush = jnp.where(cnt > 0, (rem + cnt) // PIECE, 0)
    bcast = lambda a, dt: jnp.broadcast_to(a[:, :, None].astype(dt), a.shape + (LANES,))
    return {
        "abase": (first - rem).reshape(-1), "npiece": (span // PIECE).reshape(-1), "off": off.reshape(-1),
        "nflush": nflush.reshape(-1), "rem": rem.reshape(-1), "nch": nch, "rall": rall,
        "nch_nflush": jnp.concatenate([nch, jnp.sum(nflush, axis=1)]),
        "ceq_v": bcast(ceq, F32), "pos_v": bcast(off + rem, I32),
    }


def kernel(x_prompt, x_sample, rel_bias, norm1_g, w_in, attn_sink, conv_w, conv_b, w_rec_a, b_rec_a,
           w_rec_x, b_rec_x, lru_lambda, grp_g_attn, grp_g_lru, w_out, norm2_g, w_router,
           w_e_gate, w_e_up, w_e_down, final_g):
    bp, seq, _ = x_prompt.shape
    bs = x_sample.shape[0]
    assert x_sample.shape[1] == seq
    nseq = bp + bs
    n_p, n_s = bp * seq, bs * seq
    n = n_p + n_s
    assert seq % ATTN_TQ == 0 and n_p % ROW_TILE == 0 and n_s % ROW_TILE == 0
    assert n_p % ROUTE_TILE == 0 and n_s % ROUTE_TILE == 0
    depth = w_in.shape[0]

    groups = ((0, n_p, EC_CAPACITY * n_p // N_EXPERTS), (n_p, n, EC_CAPACITY * n_s // N_EXPERTS))
    nt = n // ROUTE_TILE
    nt_p = n_p // ROUTE_TILE
    tile_grp = np.where(np.arange(nt) < nt_p, 0, 1)
    group_first_tile = np.where(np.arange(nt) < nt_p, 0, nt_p)
    real_rows = groups[0][2] + groups[1][2]
    assert real_rows % PIECE == 0
    cap_rows = -(-real_rows // EXPERT_TILE) * EXPERT_TILE
    tile_group = lambda i: (i >= nt_p).astype(I32)

    bias_tbl = _bias_table(rel_bias)
    row = lambda v: v.reshape(1, -1).astype(F32)
    x = None
    w_in_b, w_out_b = w_in.astype(BF16), w_out.astype(BF16)
    w_gate_b, w_up_b, w_down_b = w_e_gate.astype(BF16), w_e_up.astype(BF16), w_e_down.astype(BF16)

    for l in range(depth):
        if l == 0:
            qkv, xy, x = _inproj_first(x_prompt.reshape(n_p, D_MODEL), x_sample.reshape(n_s, D_MODEL),
                                       row(norm1_g[l]), w_in_b)
        else:
            qkv, xy = _inproj(x, row(norm1_g[l]), w_in_b, l)
        sink_tbl = jnp.broadcast_to(attn_sink[l].astype(F32)[:, None, None], (N_HEADS, 1, LANES))
        attn = _attention(qkv, bias_tbl, sink_tbl, nseq, seq).reshape(n, ATTN_W)
        w_rec = jnp.concatenate([w_rec_a[l], w_rec_x[l]], axis=-1).astype(BF16)
        lru = _lru(xy, conv_w[l].astype(F32), conv_b[l].astype(F32), w_rec, b_rec_a[l].astype(F32),
                   b_rec_x[l].astype(F32), lru_lambda[l].astype(F32), nseq, seq).reshape(n, LRU_W)
        x1, h2, aff_t = _outproj(attn, lru, x, row(grp_g_attn[l]), row(grp_g_lru[l]),
                                 w_out_b, l, row(norm2_g[l]), _split_router(w_router[l]))
        thr, need, gtc, eqc = _thresholds(aff_t, groups)
        meta = _route_meta(gtc, eqc, need, tile_grp, group_first_tile)
        xe, gpos = _dispatch(meta, h2, aff_t, thr, need, tile_group, cap_rows, real_rows)
        ye = _experts(xe, w_gate_b, w_up_b, w_down_b, l, cap_rows)
        if l < depth - 1:
            x = _combine(meta, x1, gpos, ye, cap_rows)
        else:
            y_p, y_s = _combine(meta, x1, gpos, ye, cap_rows, final=(row(final_g), n_p))

    return (y_p.reshape(bp, seq, D_MODEL), y_s.reshape(bs, seq, D_MODEL))
```

```python
import functools

import numpy as np
import jax
import jax.numpy as jnp
from jax import lax
from jax.experimental import pallas as pl
from jax.experimental.pallas import tpu as pltpu

F32 = jnp.float32
BF16 = jnp.bfloat16
I32 = jnp.int32

D_MODEL = 2048
HEAD_DIM = 128
N_HEADS = 8
N_KV_HEADS = 2
GQA_GROUP = N_HEADS // N_KV_HEADS
ATTN_W = N_HEADS * HEAD_DIM
KV_W = N_KV_HEADS * HEAD_DIM
QKV_W = ATTN_W + 2 * KV_W
WINDOW = 128
BLOCK = 128
N_BUCKETS = 32
MAX_DISTANCE = 128
LRU_W = D_MODEL - ATTN_W
LRU_BLOCKS = 8
LRU_BLK = LRU_W // LRU_BLOCKS
CONV_W = 4
LRU_C = 8.0
N_EXPERTS = 16
EC_CAPACITY = 2
D_EXPERT = D_MODEL // 2
EPS = 1e-6
NEG = -1e30
LOG2E = 1.4426950408889634

LANES = 128
SUBLANES = 8
BF16_ROWS = 16
MXU_DIM = 256
VMEM_LIMIT = 56 * 1024 * 1024

ROW_TILE = 512
ATTN_TQ = 1024
ATTN_SKEW = 2
LRU_FINISH = 256
LRU_SEG = 60
ROUTE_TILE = 256
PIECE = BF16_ROWS
EXPERT_TILE = 512
GATE_COLS = LANES
XE_W = D_MODEL + GATE_COLS


def _params(sem, **kw):
    return pltpu.CompilerParams(dimension_semantics=sem, vmem_limit_bytes=VMEM_LIMIT, **kw)


def _rms(x, g):
    return x * lax.rsqrt(jnp.mean(x * x, axis=-1, keepdims=True) + EPS) * g


def _inproj_kernel(x_ref, g_ref, w_ref, qkv_ref, xy_ref):
    h = _rms(x_ref[...], g_ref[...]).astype(BF16)
    qkv_ref[...] = jnp.dot(h, w_ref[:, :QKV_W], preferred_element_type=F32).astype(BF16)
    xy_ref[...] = jnp.dot(h, w_ref[:, QKV_W:], preferred_element_type=F32)


def _layer_weight(w, l):
    return pl.BlockSpec((None,) + w.shape[1:], lambda i: (l, 0, 0), pipeline_mode=pl.Buffered(1))


def _inproj(x, g, w, l):
    n = x.shape[0]
    return pl.pallas_call(
        _inproj_kernel,
        grid=(n // ROW_TILE,),
        in_specs=[
            pl.BlockSpec((ROW_TILE, D_MODEL), lambda i: (i, 0)),
            pl.BlockSpec((1, D_MODEL), lambda i: (0, 0)),
            _layer_weight(w, l),
        ],
        out_specs=[
            pl.BlockSpec((ROW_TILE, QKV_W), lambda i: (i, 0)),
            pl.BlockSpec((ROW_TILE, 2 * LRU_W), lambda i: (i, 0)),
        ],
        out_shape=[
            jax.ShapeDtypeStruct((n, QKV_W), BF16),
            jax.ShapeDtypeStruct((n, 2 * LRU_W), F32),
        ],
        compiler_params=_params(("arbitrary",)),
    )(x, g, w)


def _inproj_first_kernel(xp_ref, xs_ref, g_ref, w_ref, qkv_ref, xy_ref, x_ref, *, tiles_p):
    i = pl.program_id(0)

    def run(src):
        x_ref[...] = src[...]
        _inproj_kernel(src, g_ref, w_ref, qkv_ref, xy_ref)

    pl.when(i < tiles_p)(lambda: run(xp_ref))
    pl.when(i >= tiles_p)(lambda: run(xs_ref))


def _inproj_first(xp, xs, g, w):
    n_p, n_s = xp.shape[0], xs.shape[0]
    n = n_p + n_s
    tile = ROW_TILE // 2
    tiles_p = n_p // tile
    row = lambda w_: pl.BlockSpec((tile, w_), lambda i: (i, 0))
    return pl.pallas_call(
        functools.partial(_inproj_first_kernel, tiles_p=tiles_p),
        grid=(n // tile,),
        in_specs=[
            pl.BlockSpec((tile, D_MODEL), lambda i: (jnp.minimum(i, tiles_p - 1), 0)),
            pl.BlockSpec((tile, D_MODEL), lambda i: (jnp.maximum(i - tiles_p, 0), 0)),
            pl.BlockSpec((1, D_MODEL), lambda i: (0, 0)),
            _layer_weight(w, 0),
        ],
        out_specs=[row(QKV_W), row(2 * LRU_W), row(D_MODEL)],
        out_shape=[
            jax.ShapeDtypeStruct((n, QKV_W), BF16),
            jax.ShapeDtypeStruct((n, 2 * LRU_W), F32),
            jax.ShapeDtypeStruct((n, D_MODEL), F32),
        ],
        compiler_params=_params(("arbitrary",)),
    )(xp, xs, g, w)


def _attn_kernel(q_ref, kp_ref, km_ref, kn_ref, vp_ref, vm_ref, vn_ref, bias_ref, sink_ref, o_ref,
                 s_scr, p_scr, m_scr, d_scr, *, seq):
    i = pl.program_id(2)
    kwin = jnp.concatenate([kp_ref[0], km_ref[0], kn_ref[0]], axis=0)
    vwin = jnp.concatenate([vp_ref[0], vm_ref[0], vn_ref[0]], axis=0)
    qi = lax.broadcasted_iota(I32, (BLOCK, 3 * BLOCK), 0)
    ji = lax.broadcasted_iota(I32, (BLOCK, 3 * BLOCK), 1)
    band = jnp.abs(ji - BLOCK - qi) <= WINDOW
    scale = HEAD_DIM ** -0.5 * LOG2E
    nsb = ATTN_TQ // BLOCK
    unit = lambda sb, g: sb * GQA_GROUP + g
    qrows = lambda sb: slice(sb * BLOCK, (sb + 1) * BLOCK)
    hcols = lambda g: slice(g * HEAD_DIM, (g + 1) * HEAD_DIM)
    sink = [sink_ref[g][:, :1] for g in range(GQA_GROUP)]
    ones = jnp.ones((3 * BLOCK, HEAD_DIM), BF16)
    valid, vext = [], []
    for sb in range(nsb):
        kpos = i * ATTN_TQ + (sb - 1) * BLOCK + ji
        valid.append(band & (kpos >= 0) & (kpos < seq))
        vext.append(jnp.concatenate([vwin[sb * BLOCK:sb * BLOCK + 3 * BLOCK], ones], axis=1))

    def scores(sb, g):
        s = lax.dot_general(q_ref[0, qrows(sb), hcols(g)], kwin[sb * BLOCK:sb * BLOCK + 3 * BLOCK],
                            (((1,), (1,)), ((), ())), preferred_element_type=F32)
        s_scr[unit(sb, g)] = jnp.where(valid[sb], s * scale + bias_ref[g], NEG)

    def row_max(sb, g):
        u = unit(sb, g)
        m_scr[u] = jnp.maximum(jnp.max(s_scr[u], axis=-1, keepdims=True), sink[g])

    def exponent(sb, g):
        u = unit(sb, g)
        p_scr[u] = jnp.exp2(s_scr[u] - m_scr[u]).astype(BF16)

    def values(sb, g):
        u = unit(sb, g)
        o = jnp.dot(p_scr[u], vext[sb], preferred_element_type=F32)
        denom = o[:, HEAD_DIM:HEAD_DIM + 1] + jnp.exp2(sink[g] - m_scr[u])
        o_ref[0, qrows(sb), hcols(g)] = o[:, :HEAD_DIM] / denom

    passes = (scores, row_max, exponent, values)
    units = [(sb, g) for sb in range(nsb) for g in range(GQA_GROUP)]
    for t in range(len(units) + ATTN_SKEW * (len(passes) - 1)):
        for k, fn in enumerate(passes):
            if 0 <= t - ATTN_SKEW * k < len(units):
                fn(*units[t - ATTN_SKEW * k])


def _attention(qkv, bias_tbl, sink_tbl, nseq, seq):
    qkv3 = qkv.reshape(nseq, seq, QKV_W)
    per = ATTN_TQ // BLOCK
    last = seq // BLOCK - 1
    kcol = ATTN_W // HEAD_DIM
    vcol = (ATTN_W + KV_W) // HEAD_DIM

    def side(col, prev):
        if prev:
            return pl.BlockSpec((1, BLOCK, HEAD_DIM), lambda b, k, i: (b, jnp.maximum(i * per - 1, 0), col + k))
        return pl.BlockSpec((1, BLOCK, HEAD_DIM), lambda b, k, i: (b, jnp.minimum(i * per + per, last), col + k))

    def main(col):
        return pl.BlockSpec((1, ATTN_TQ, HEAD_DIM), lambda b, k, i: (b, i, col + k))

    qw = GQA_GROUP * HEAD_DIM
    return pl.pallas_call(
        functools.partial(_attn_kernel, seq=seq),
        grid=(nseq, N_KV_HEADS, seq // ATTN_TQ),
        in_specs=[
            pl.BlockSpec((1, ATTN_TQ, qw), lambda b, k, i: (b, i, k)),
            side(kcol, True), main(kcol), side(kcol, False),
            side(vcol, True), main(vcol), side(vcol, False),
            pl.BlockSpec((GQA_GROUP, BLOCK, 3 * BLOCK), lambda b, k, i: (k, 0, 0)),
            pl.BlockSpec((GQA_GROUP, 1, LANES), lambda b, k, i: (k, 0, 0)),
        ],
        out_specs=pl.BlockSpec((1, ATTN_TQ, qw), lambda b, k, i: (b, i, k)),
        out_shape=jax.ShapeDtypeStruct((nseq, seq, ATTN_W), F32),
        scratch_shapes=[
            pltpu.VMEM((per * GQA_GROUP, BLOCK, 3 * BLOCK), F32),
            pltpu.VMEM((per * GQA_GROUP, BLOCK, 3 * BLOCK), BF16),
            pltpu.VMEM((per * GQA_GROUP, BLOCK, 1), F32),
            pltpu.VMEM((per * GQA_GROUP, BLOCK, 1), F32),
        ],
        compiler_params=_params(("arbitrary", "arbitrary", "arbitrary")),
    )(qkv3, qkv3, qkv3, qkv3, qkv3, qkv3, qkv3, bias_tbl, sink_tbl)


def _t5_bucket(rel):
    nb = N_BUCKETS // 2
    ret = (rel > 0).astype(np.int32) * nb
    n = np.abs(rel)
    max_exact = nb // 2
    large = max_exact + (np.log(np.maximum(n, 1) / max_exact) / np.log(MAX_DISTANCE / max_exact)
                         * (nb - max_exact)).astype(np.int32)
    large = np.minimum(large, nb - 1)
    return (ret + np.where(n < max_exact, n, large)).astype(np.int32)


def _bias_table(rel_bias):
    rel = (np.arange(3 * BLOCK)[None, :] - BLOCK) - np.arange(BLOCK)[:, None]
    onehot = np.eye(N_BUCKETS, dtype=np.float32)[_t5_bucket(rel)]
    return jnp.einsum("qkb,bh->hqk", onehot, rel_bias.astype(F32), precision=lax.Precision.HIGHEST)


def _sublane_scan(a, u, reverse):
    sub = lax.broadcasted_iota(I32, a.shape, 0)
    for d in (1, 2, 4):
        keep = (sub < SUBLANES - d) if reverse else (sub >= d)
        shift = SUBLANES - d if reverse else d
        a_sh = jnp.where(keep, pltpu.roll(a, shift, 0), 1.0)
        u_sh = jnp.where(keep, pltpu.roll(u, shift, 0), 0.0)
        u = a * u_sh + u
        a = a * a_sh
    return a, u


def _lru_chunk(xr_ref, base, seg, d, prm, state):
    cw, cb, w, ba, bx, decay = prm
    halo, carry = state[:3], state[3]
    sub = lax.broadcasted_iota(I32, (SUBLANES, LRU_BLK), 0)
    x = [xr_ref[0, pl.ds(base + j, SUBLANES, stride=seg), :] for j in range(seg)]
    if d == 0:
        edge = [jnp.where(sub == 0, halo[m], pltpu.roll(x[seg - 1 - m], 1, 0)) for m in range(3)]
        at = lambda j: x[j] if j >= 0 else edge[-j - 1]
        taps = lambda j: [at(j - k) for k in range(CONV_W)]
        new_halo = [x[seg - 1 - m][SUBLANES - 1:SUBLANES] for m in range(3)]
    else:
        edge = [jnp.where(sub == SUBLANES - 1, halo[m], pltpu.roll(x[m], SUBLANES - 1, 0)) for m in range(3)]
        at = lambda j: x[j] if j < seg else edge[j - seg]
        taps = lambda j: [at(j + k) for k in range(CONV_W)]
        new_halo = [x[m][0:1] for m in range(3)]
    xc = []
    for j in range(seg):
        t = taps(j)
        acc = cb[d] + cw[d][CONV_W - 1] * t[0]
        for k in range(1, CONV_W):
            acc = acc + cw[d][CONV_W - 1 - k] * t[k]
        xc.append(acc)
    xc_all = jnp.concatenate(xc, axis=0)
    z = jnp.dot(xc_all.astype(BF16), w[d], preferred_element_type=F32)
    r = jax.nn.sigmoid(z[:, :LRU_BLK] + ba[d])
    ig = jax.nn.sigmoid(z[:, LRU_BLK:] + bx[d])
    log_a = decay[d] * r
    a_all = jnp.exp(log_a)
    th = jnp.tanh(log_a)
    u_all = jnp.sqrt(-2.0 * th / (1.0 - th)) * (ig * xc_all)
    a = [a_all[j * SUBLANES:(j + 1) * SUBLANES] for j in range(seg)]
    u = [u_all[j * SUBLANES:(j + 1) * SUBLANES] for j in range(seg)]
    order = list(range(seg)) if d == 0 else list(range(seg - 1, -1, -1))
    hl, pr = [None] * seg, [None] * seg
    prev = None
    for j in order:
        if prev is None:
            hl[j], pr[j] = u[j], a[j]
        else:
            hl[j] = a[j] * hl[prev] + u[j]
            pr[j] = a[j] * pr[prev]
        prev = j
    ai, ui = _sublane_scan(pr[prev], hl[prev], d == 1)
    end = ui + ai * carry
    if d == 0:
        start = jnp.where(sub == 0, carry, pltpu.roll(end, 1, 0))
        new_carry = end[SUBLANES - 1:SUBLANES]
    else:
        start = jnp.where(sub == SUBLANES - 1, carry, pltpu.roll(end, SUBLANES - 1, 0))
        new_carry = end[0:1]
    h = [hl[j] + pr[j] * start for j in range(seg)]
    return h, tuple(new_halo) + (new_carry,)


def _lru_kernel(xr_ref, yr_ref, cw_ref, cb_ref, w_ref, ba_ref, bx_ref, lam_ref, o_ref, hb_ref, *, seq):
    chunk = SUBLANES * LRU_SEG
    nfull = seq // chunk
    tail_seg = (seq - nfull * chunk) // SUBLANES
    prm = (
        [[cw_ref[d, j:j + 1, :] for j in range(CONV_W)] for d in range(2)],
        [cb_ref[d] for d in range(2)],
        [w_ref[d, 0] for d in range(2)],
        [ba_ref[d] for d in range(2)],
        [bx_ref[d] for d in range(2)],
        [-LRU_C * jax.nn.softplus(-lam_ref[d]) for d in range(2)],
    )
    rows = lambda base, j, seg: (0, pl.ds(base + j, SUBLANES, stride=seg), slice(None))

    def scan(dst_ref, base, seg, d, state):
        h, state = _lru_chunk(xr_ref, base, seg, d, prm, state)
        for j in range(seg):
            dst_ref[rows(base, j, seg)] = h[j]
        return state

    def both(k, st):
        fwd = scan(o_ref, k * chunk, LRU_SEG, 0, st[0])
        bwd = scan(hb_ref, (nfull - 1 - k) * chunk, LRU_SEG, 1, st[1])
        return (fwd, bwd)

    zero = jnp.zeros((1, LRU_BLK), F32)
    bwd = (zero,) * 4
    if tail_seg:
        bwd = scan(hb_ref, nfull * chunk, tail_seg, 1, bwd)
    fwd, _ = lax.fori_loop(0, nfull, both, ((zero,) * 4, bwd))
    if tail_seg:
        scan(o_ref, nfull * chunk, tail_seg, 0, fwd)

    def finish(k, _):
        blk = (0, pl.ds(pl.multiple_of(k * LRU_FINISH, LRU_FINISH), LRU_FINISH), slice(None))
        o_ref[blk] = (o_ref[blk] + hb_ref[blk]) * jax.nn.gelu(yr_ref[blk])
        return 0

    lax.fori_loop(0, seq // LRU_FINISH, finish, 0)


def _lru(xy, cw, cb, w, ba, bx, lam, nseq, seq):
    tail = seq % (SUBLANES * LRU_SEG)
    assert tail % SUBLANES == 0 and (tail == 0 or tail // SUBLANES >= CONV_W - 1) and seq % LRU_FINISH == 0
    xy3 = xy.reshape(nseq, seq, 2 * LRU_W)
    vec = lambda: pl.BlockSpec((2, 1, LRU_BLK), lambda b, c: (0, 0, c))
    return pl.pallas_call(
        functools.partial(_lru_kernel, seq=seq),
        grid=(nseq, LRU_BLOCKS),
        in_specs=[
            pl.BlockSpec((1, seq, LRU_BLK), lambda b, c: (b, 0, c)),
            pl.BlockSpec((1, seq, LRU_BLK), lambda b, c: (b, 0, LRU_BLOCKS + c)),
            pl.BlockSpec((2, CONV_W, LRU_BLK), lambda b, c: (0, 0, c)),
            vec(),
            pl.BlockSpec((2, 1, LRU_BLK, 2 * LRU_BLK), lambda b, c: (0, c, 0, 0)),
            vec(), vec(), vec(),
        ],
        out_specs=pl.BlockSpec((1, seq, LRU_BLK), lambda b, c: (b, 0, c)),
        out_shape=jax.ShapeDtypeStruct((nseq, seq, LRU_W), F32),
        scratch_shapes=[pltpu.VMEM((1, seq, LRU_BLK), F32)],
        compiler_params=_params(("arbitrary", "arbitrary")),
    )(xy3, xy3, cw, cb.reshape(2, 1, LRU_W), w, ba.reshape(2, 1, LRU_W),
      bx.reshape(2, 1, LRU_W), lam.reshape(2, 1, LRU_W))


def _outproj_kernel(attn_ref, lru_ref, x_ref, ga_ref, gl_ref, w_ref, g2_ref, wr_ref, x1_ref, h2_ref, aff_ref):
    an = _rms(attn_ref[...], ga_ref[...]).astype(BF16)
    ln = _rms(lru_ref[...], gl_ref[...]).astype(BF16)
    y = jnp.dot(an, w_ref[:ATTN_W], preferred_element_type=F32)
    y = y + jnp.dot(ln, w_ref[ATTN_W:], preferred_element_type=F32)
    x1 = x_ref[...] + y
    x1_ref[...] = x1
    hf = _rms(x1, g2_ref[...])
    h_hi = hf.astype(BF16)
    h2_ref[...] = h_hi
    h_lo = (hf - h_hi.astype(F32)).astype(BF16)
    nt = (((1,), (1,)), ((), ()))
    a = lax.dot_general(wr_ref[...], h_hi, nt, preferred_element_type=F32)
    b = lax.dot_general(wr_ref[:N_EXPERTS], h_lo, nt, preferred_element_type=F32)
    logits = a[:N_EXPERTS] + a[N_EXPERTS:] + b
    e = jnp.exp(logits - jnp.max(logits, axis=0, keepdims=True))
    aff_ref[...] = e / jnp.sum(e, axis=0, keepdims=True)


def _outproj(attn, lru, x, ga, gl, w, l, g2, wr_t):
    n = x.shape[0]
    row = lambda w_: pl.BlockSpec((ROW_TILE, w_), lambda i: (i, 0))
    full = lambda a: pl.BlockSpec(a.shape, lambda i: (0,) * a.ndim)
    return pl.pallas_call(
        _outproj_kernel,
        grid=(n // ROW_TILE,),
        in_specs=[row(ATTN_W), row(LRU_W), row(D_MODEL), full(ga), full(gl),
                  _layer_weight(w, l), full(g2), full(wr_t)],
        out_specs=[row(D_MODEL), row(D_MODEL), pl.BlockSpec((N_EXPERTS, ROW_TILE), lambda i: (0, i))],
        out_shape=[
            jax.ShapeDtypeStruct((n, D_MODEL), F32),
            jax.ShapeDtypeStruct((n, D_MODEL), BF16),
            jax.ShapeDtypeStruct((N_EXPERTS, n), F32),
        ],
        compiler_params=_params(("arbitrary",)),
    )(attn, lru, x, ga, gl, w, g2, wr_t)


def _split_router(w_router):
    wt = w_router.astype(F32).T
    hi = wt.astype(BF16)
    lo = (wt - hi.astype(F32)).astype(BF16)
    return jnp.concatenate([hi, lo], axis=0)


def _threshold_kernel(aff_ref, thr_ref, need_ref, gt_ref, eq_ref, *, groups):
    for gi, (lo, hi, cap) in enumerate(groups):
        def keys():
            return lax.bitcast_convert_type(aff_ref[:, lo:hi], I32)

        def step(it, thr):
            cand = thr | jnp.left_shift(jnp.int32(1), 30 - it)
            cnt = jnp.sum((keys() >= cand).astype(F32), axis=1, keepdims=True)
            return jnp.where(cnt >= cap, cand, thr)

        thr = lax.fori_loop(0, 31, step, jnp.zeros((N_EXPERTS, 1), I32))
        ngt = jnp.sum((keys() > thr).astype(F32), axis=1, keepdims=True)
        thr_ref[gi] = jnp.broadcast_to(thr, (N_EXPERTS, LANES))
        need_ref[gi] = jnp.broadcast_to(cap - ngt, (N_EXPERTS, LANES))
        for t in range(lo // ROUTE_TILE, hi // ROUTE_TILE):
            kt = lax.bitcast_convert_type(aff_ref[:, t * ROUTE_TILE:(t + 1) * ROUTE_TILE], I32)
            gt_ref[t] = jnp.broadcast_to(jnp.sum((kt > thr).astype(F32), axis=1, keepdims=True), (N_EXPERTS, LANES))
            eq_ref[t] = jnp.broadcast_to(jnp.sum((kt == thr).astype(F32), axis=1, keepdims=True), (N_EXPERTS, LANES))


def _thresholds(aff_t, groups):
    n = aff_t.shape[1]
    nt = n // ROUTE_TILE
    ng = len(groups)
    return pl.pallas_call(
        functools.partial(_threshold_kernel, groups=groups),
        out_shape=[
            jax.ShapeDtypeStruct((ng, N_EXPERTS, LANES), I32),
            jax.ShapeDtypeStruct((ng, N_EXPERTS, LANES), F32),
            jax.ShapeDtypeStruct((nt, N_EXPERTS, LANES), F32),
            jax.ShapeDtypeStruct((nt, N_EXPERTS, LANES), F32),
        ],
        compiler_params=pltpu.CompilerParams(vmem_limit_bytes=VMEM_LIMIT),
    )(aff_t)


def _segment_pieces(tile, npiece_s, off_s, fn):
    for e in range(N_EXPERTS):
        npiece = npiece_s[tile * N_EXPERTS + e]
        first = off_s[tile * N_EXPERTS + e]

        def body(j, _, e=e, first=first):
            fn(e, j, pl.multiple_of(first + j * PIECE, PIECE))
            return 0

        lax.fori_loop(0, npiece, body, 0)


def _wait_pieces(piece_copy, count):
    def body(j, _):
        piece_copy.wait()
        return 0

    lax.fori_loop(0, count, body, 0)


def _piece_onehot(gpos, e, row):
    rid = lax.broadcasted_iota(I32, (PIECE, gpos.shape[1]), 0) + row
    return gpos[e:e + 1, :] == rid


def _dispatch_kernel(abase_s, npiece_s, off_s, nch_s, nflush_s, rem_s,
                     h2_ref, aff_ref, thr_ref, need_ref, ceq_ref, offv_ref,
                     xe_ref, gpos_ref, stage, pbuf, carry, zeros, tri, sem, *, tile_group, cap_rows, real_rows):
    i = pl.program_id(0)
    tm = ROUTE_TILE

    @pl.when(i == 0)
    def _():
        pbuf[...] = jnp.zeros(pbuf.shape, BF16)
        earlier = lax.broadcasted_iota(I32, (tm, tm), 0) < lax.broadcasted_iota(I32, (tm, tm), 1)
        tri[...] = jnp.where(earlier, 1.0, 0.0).astype(BF16)

    grp = tile_group(i)
    thr = jnp.where(grp == 0, thr_ref[0], thr_ref[1])[:, :1]
    need = jnp.where(grp == 0, need_ref[0], need_ref[1])[:, :1]
    aff = aff_ref[...]
    keys = lax.bitcast_convert_type(aff, I32)
    gt = keys > thr
    eq = keys == thr
    before = tri[...]
    eq_rank = jnp.dot(jnp.where(eq, 1.0, 0.0).astype(BF16), before, preferred_element_type=F32)
    sel = gt | (eq & (ceq_ref[0][:, :1] + eq_rank < need))
    rank = jnp.dot(jnp.where(sel, 1.0, 0.0).astype(BF16), before, preferred_element_type=F32)
    gpos = jnp.where(sel, offv_ref[0][:, :1] + rank.astype(I32), -1)
    gpos_ref[...] = gpos

    g_hi = aff.astype(BF16).astype(F32)
    g_mid = (aff - g_hi).astype(BF16).astype(F32)
    g_lo = (aff - g_hi - g_mid).astype(BF16).astype(F32)
    pad = jnp.zeros((GATE_COLS - 3 * N_EXPERTS, tm), F32)
    gate_cols = jnp.concatenate([g_hi, g_mid, g_lo, pad], axis=0).T.astype(BF16)

    def build(e, j, row):
        pbuf[pl.ds(row, PIECE), :] = jnp.where(_piece_onehot(gpos, e, row), 1.0, 0.0).astype(BF16)

    _segment_pieces(i, npiece_s, off_s, build)

    nt = pl.num_programs(0)
    slot = lax.rem(i, 2)
    mine = stage.at[slot]

    def chunk(c, _):
        r0 = pl.multiple_of(c * MXU_DIM, MXU_DIM)
        onehot = pbuf[pl.ds(r0, MXU_DIM), :]
        rows = jnp.dot(onehot, h2_ref[...], preferred_element_type=F32)
        mine[pl.ds(r0, MXU_DIM), pl.ds(0, D_MODEL)] = rows.astype(BF16)
        gates = jnp.dot(onehot, gate_cols, preferred_element_type=F32)
        mine[pl.ds(r0, MXU_DIM), pl.ds(D_MODEL, GATE_COLS)] = gates.astype(BF16)
        return 0

    lax.fori_loop(0, nch_s[i], chunk, 0)

    for e in range(N_EXPERTS):
        k = i * N_EXPERTS + e

        @pl.when((npiece_s[k] > 0) & (rem_s[k] > 0))
        def _(e=e, k=k):
            head = pl.ds(pl.multiple_of(off_s[k], PIECE), PIECE)
            mine[head, :] = mine[head, :] + carry[e]

    def copy(s, e, j, row):
        dst = pl.multiple_of(e * cap_rows + abase_s[i * N_EXPERTS + e] + j * PIECE, PIECE)
        return pltpu.make_async_copy(stage.at[s, pl.ds(row, PIECE)], xe_ref.at[pl.ds(dst, PIECE)], sem.at[s])

    _segment_pieces(i, nflush_s, off_s, lambda e, j, row: copy(slot, e, j, row).start())

    for e in range(N_EXPERTS):
        k = i * N_EXPERTS + e

        @pl.when(npiece_s[k] > nflush_s[k])
        def _(e=e, k=k):
            carry[e] = mine[pl.ds(pl.multiple_of(off_s[k] + nflush_s[k] * PIECE, PIECE), PIECE), :]

    @pl.when(i > 0)
    def _():
        _wait_pieces(copy(1 - slot, 0, 0, 0), nch_s[nt + i - 1])

    @pl.when(i == nt - 1)
    def _():
        _wait_pieces(copy(slot, 0, 0, 0), nch_s[nt + i])

    ntail = (cap_rows - real_rows) // PIECE
    if ntail:
        @pl.when(i == nt - 1)
        def _():
            zeros[...] = jnp.zeros(zeros.shape, BF16)
            fills = [pltpu.make_async_copy(zeros, xe_ref.at[pl.ds(e * cap_rows + real_rows + j * PIECE, PIECE)],
                                           sem.at[0])
                     for e in range(N_EXPERTS) for j in range(ntail)]
            for f in fills:
                f.start()
            for f in fills:
                f.wait()


def _dispatch(meta, h2, aff_t, thr, need, tile_group, cap_rows, real_rows):
    n = h2.shape[0]
    nt = n // ROUTE_TILE
    tile3 = lambda: pl.BlockSpec((1, N_EXPERTS, LANES), lambda i, *_: (i, 0, 0))
    full3 = lambda a: pl.BlockSpec(a.shape, lambda i, *_: (0, 0, 0))
    grid_spec = pltpu.PrefetchScalarGridSpec(
        num_scalar_prefetch=6,
        grid=(nt,),
        in_specs=[
            pl.BlockSpec((ROUTE_TILE, D_MODEL), lambda i, *_: (i, 0)),
            pl.BlockSpec((N_EXPERTS, ROUTE_TILE), lambda i, *_: (0, i)),
            full3(thr), full3(need), tile3(), tile3(),
        ],
        out_specs=[
            pl.BlockSpec(memory_space=pl.ANY),
            pl.BlockSpec((N_EXPERTS, ROUTE_TILE), lambda i, *_: (0, i)),
        ],
        scratch_shapes=[
            pltpu.VMEM((2, N_EXPERTS * ROUTE_TILE, XE_W), BF16),
            pltpu.VMEM((N_EXPERTS * ROUTE_TILE, ROUTE_TILE), BF16),
            pltpu.VMEM((N_EXPERTS, PIECE, XE_W), BF16),
            pltpu.VMEM((PIECE, XE_W), BF16),
            pltpu.VMEM((ROUTE_TILE, ROUTE_TILE), BF16),
            pltpu.SemaphoreType.DMA((2,)),
        ],
    )
    return pl.pallas_call(
        functools.partial(_dispatch_kernel, tile_group=tile_group, cap_rows=cap_rows, real_rows=real_rows),
        grid_spec=grid_spec,
        out_shape=[
            jax.ShapeDtypeStruct((N_EXPERTS * cap_rows, XE_W), BF16),
            jax.ShapeDtypeStruct((N_EXPERTS, n), I32),
        ],
        compiler_params=_params(("arbitrary",)),
    )(meta["abase"], meta["npiece"], meta["off"], meta["nch_nflush"], meta["nflush"], meta["rem"],
      h2, aff_t, thr, need, meta["ceq_v"], meta["pos_v"])


def _expert_kernel(xe_ref, wg_ref, wu_ref, wd_ref, ye_ref):
    e = pl.program_id(0)
    x = xe_ref[:, :D_MODEL]
    gc = xe_ref[:, D_MODEL:].astype(F32)
    lane = lax.broadcasted_iota(I32, gc.shape, 1)
    gate = jnp.sum(jnp.where((lane & (N_EXPERTS - 1)) == e, gc, 0.0), axis=1, keepdims=True)
    hg = jnp.dot(x, wg_ref[0], preferred_element_type=F32)
    hu = jnp.dot(x, wu_ref[0], preferred_element_type=F32)
    hid = (jax.nn.silu(hg) * hu).astype(BF16)
    y = jnp.dot(hid, wd_ref[0], preferred_element_type=F32) * gate
    ye_ref[...] = y.astype(BF16)


def _experts(xe, wg, wu, wd, l, cap_rows):
    rt = cap_rows // EXPERT_TILE
    rows = lambda w_: pl.BlockSpec((EXPERT_TILE, w_), lambda e, r: (e * rt + r, 0))
    weight = lambda w: pl.BlockSpec((None, 1) + w.shape[2:], lambda e, r: (l, e, 0, 0))
    return pl.pallas_call(
        _expert_kernel,
        grid=(N_EXPERTS, rt),
        in_specs=[rows(XE_W), weight(wg), weight(wu), weight(wd)],
        out_specs=rows(D_MODEL),
        out_shape=jax.ShapeDtypeStruct((N_EXPERTS * cap_rows, D_MODEL), BF16),
        compiler_params=_params(("arbitrary", "arbitrary")),
    )(xe, wg, wu, wd)


def _combine_kernel(abase_s, npiece_s, off_s, nch_s, rall_s, x1_ref, gpos_ref, ye_ref, x2_ref,
                    stage, pbuf, sem, *, cap_rows):
    i = pl.program_id(0)
    nt = pl.num_programs(0)
    slot = lax.rem(i, 2)

    def copy(t, s, e, j, row):
        src = pl.multiple_of(e * cap_rows + abase_s[t * N_EXPERTS + e] + j * PIECE, PIECE)
        return pltpu.make_async_copy(ye_ref.at[pl.ds(src, PIECE)], stage.at[s, pl.ds(row, PIECE)], sem.at[s])

    def fetch(t, s):
        _segment_pieces(t, npiece_s, off_s, lambda e, j, row: copy(t, s, e, j, row).start())

    @pl.when(i == 0)
    def _():
        stage[...] = jnp.zeros(stage.shape, BF16)
        fetch(0, 0)

    @pl.when(i + 1 < nt)
    def _():
        fetch(i + 1, 1 - slot)

    gpos = gpos_ref[...]

    def build(e, j, row):
        pbuf[pl.ds(row, PIECE), :] = jnp.where(_piece_onehot(gpos, e, row), 1.0, 0.0).astype(BF16)

    _segment_pieces(i, npiece_s, off_s, build)

    def clear(j, _):
        row = pl.multiple_of(rall_s[i] + j * PIECE, PIECE)
        pbuf[pl.ds(row, PIECE), :] = jnp.zeros((PIECE, ROUTE_TILE), BF16)
        return 0

    lax.fori_loop(0, (nch_s[i] * MXU_DIM - rall_s[i]) // PIECE, clear, 0)
    _wait_pieces(copy(i, slot, 0, 0, 0), rall_s[i] // PIECE)
    mine = stage.at[slot]

    def chunk(c):
        r0 = pl.multiple_of(c * MXU_DIM, MXU_DIM)
        onehot_t = pbuf[pl.ds(r0, MXU_DIM), :].astype(F32).T.astype(BF16)
        return jnp.dot(onehot_t, mine[pl.ds(r0, MXU_DIM), :], preferred_element_type=F32)

    @pl.when(nch_s[i] == 0)
    def _():
        x2_ref[...] = x1_ref[...]

    @pl.when(nch_s[i] > 0)
    def _():
        x2_ref[...] = x1_ref[...] + chunk(0)

    def more(c, _):
        x2_ref[...] += chunk(c)
        return 0

    lax.fori_loop(1, nch_s[i], more, 0)


def _combine_final_kernel(abase_s, npiece_s, off_s, nch_s, rall_s, x1_ref, gpos_ref, ye_ref, g_ref,
                          yp_ref, ys_ref, stage, pbuf, sem, acc, *, cap_rows, tiles_p):
    _combine_kernel(abase_s, npiece_s, off_s, nch_s, rall_s, x1_ref, gpos_ref, ye_ref, acc,
                    stage, pbuf, sem, cap_rows=cap_rows)
    i = pl.program_id(0)
    y = _rms(acc[...], g_ref[...])

    @pl.when(i < tiles_p)
    def _():
        yp_ref[...] = y

    @pl.when(i >= tiles_p)
    def _():
        ys_ref[...] = y


def _combine(meta, x1, gpos, ye, cap_rows, final=None):
    n = x1.shape[0]
    tile = lambda: pl.BlockSpec((ROUTE_TILE, D_MODEL), lambda i, *_: (i, 0))
    in_specs = [tile(), pl.BlockSpec((N_EXPERTS, ROUTE_TILE), lambda i, *_: (0, i)),
                pl.BlockSpec(memory_space=pl.ANY)]
    scratch = [
        pltpu.VMEM((2, N_EXPERTS * ROUTE_TILE, D_MODEL), BF16),
        pltpu.VMEM((N_EXPERTS * ROUTE_TILE, ROUTE_TILE), BF16),
        pltpu.SemaphoreType.DMA((2,)),
    ]
    args = (meta["abase"], meta["npiece"], meta["off"], meta["nch"], meta["rall"], x1, gpos, ye)
    if final is None:
        body = functools.partial(_combine_kernel, cap_rows=cap_rows)
        out_specs = tile()
        out_shape = jax.ShapeDtypeStruct((n, D_MODEL), F32)
    else:
        g, n_p = final
        tiles_p = n_p // ROUTE_TILE
        body = functools.partial(_combine_final_kernel, cap_rows=cap_rows, tiles_p=tiles_p)
        in_specs.append(pl.BlockSpec((1, D_MODEL), lambda i, *_: (0, 0)))
        scratch.append(pltpu.VMEM((ROUTE_TILE, D_MODEL), F32))
        out_specs = [
            pl.BlockSpec((ROUTE_TILE, D_MODEL), lambda i, *_: (jnp.minimum(i, tiles_p - 1), 0)),
            pl.BlockSpec((ROUTE_TILE, D_MODEL), lambda i, *_: (jnp.maximum(i - tiles_p, 0), 0)),
        ]
        out_shape = [jax.ShapeDtypeStruct((n_p, D_MODEL), F32), jax.ShapeDtypeStruct((n - n_p, D_MODEL), F32)]
        args = args + (g,)
    grid_spec = pltpu.PrefetchScalarGridSpec(
        num_scalar_prefetch=5, grid=(n // ROUTE_TILE,), in_specs=in_specs, out_specs=out_specs,
        scratch_shapes=scratch)
    return pl.pallas_call(body, grid_spec=grid_spec, out_shape=out_shape,
                          compiler_params=_params(("arbitrary",)))(*args)


def _route_meta(gtc, eqc, need, tile_grp, group_first_tile):
    gtc = gtc[:, :, 0].astype(I32)
    eqc = eqc[:, :, 0].astype(I32)
    need_t = need[:, :, 0].astype(I32)[tile_grp]
    ceq = jnp.cumsum(eqc, axis=0) - eqc
    ceq = ceq - ceq[group_first_tile]
    cnt = gtc + jnp.clip(need_t - ceq, 0, eqc)
    first = jnp.cumsum(cnt, axis=0) - cnt
    rem = first % PIECE
    span = jnp.where(cnt > 0, (rem + cnt + PIECE - 1) // PIECE * PIECE, 0)
    off = jnp.cumsum(span, axis=1) - span
    rall = jnp.sum(span, axis=1)
    nch = (rall + MXU_DIM - 1) // MXU_DIM
    nflush = jnp.where(cnt > 0, (rem + cnt) // PIECE, 0)
    bcast = lambda a, dt: jnp.broadcast_to(a[:, :, None].astype(dt), a.shape + (LANES,))
    return {
        "abase": (first - rem).reshape(-1), "npiece": (span // PIECE).reshape(-1), "off": off.reshape(-1),
        "nflush": nflush.reshape(-1), "rem": rem.reshape(-1), "nch": nch, "rall": rall,
        "nch_nflush": jnp.concatenate([nch, jnp.sum(nflush, axis=1)]),
        "ceq_v": bcast(ceq, F32), "pos_v": bcast(off + rem, I32),
    }


def kernel(x_prompt, x_sample, rel_bias, norm1_g, w_in, attn_sink, conv_w, conv_b, w_rec_a, b_rec_a,
           w_rec_x, b_rec_x, lru_lambda, grp_g_attn, grp_g_lru, w_out, norm2_g, w_router,
           w_e_gate, w_e_up, w_e_down, final_g):
    bp, seq, _ = x_prompt.shape
    bs = x_sample.shape[0]
    assert x_sample.shape[1] == seq
    nseq = bp + bs
    n_p, n_s = bp * seq, bs * seq
    n = n_p + n_s
    assert seq % ATTN_TQ == 0 and n_p % ROW_TILE == 0 and n_s % ROW_TILE == 0
    assert n_p % ROUTE_TILE == 0 and n_s % ROUTE_TILE == 0
    depth = w_in.shape[0]

    groups = ((0, n_p, EC_CAPACITY * n_p // N_EXPERTS), (n_p, n, EC_CAPACITY * n_s // N_EXPERTS))
    nt = n // ROUTE_TILE
    nt_p = n_p // ROUTE_TILE
    tile_grp = np.where(np.arange(nt) < nt_p, 0, 1)
    group_first_tile = np.where(np.arange(nt) < nt_p, 0, nt_p)
    real_rows = groups[0][2] + groups[1][2]
    assert real_rows % PIECE == 0
    cap_rows = -(-real_rows // EXPERT_TILE) * EXPERT_TILE
    tile_group = lambda i: (i >= nt_p).astype(I32)

    bias_tbl = _bias_table(rel_bias) * LOG2E
    row = lambda v: v.reshape(1, -1).astype(F32)
    x = None
    w_in_b, w_out_b = w_in.astype(BF16), w_out.astype(BF16)
    w_gate_b, w_up_b, w_down_b = w_e_gate.astype(BF16), w_e_up.astype(BF16), w_e_down.astype(BF16)

    for l in range(depth):
        if l == 0:
            qkv, xy, x = _inproj_first(x_prompt.reshape(n_p, D_MODEL), x_sample.reshape(n_s, D_MODEL),
                                       row(norm1_g[l]), w_in_b)
        else:
            qkv, xy = _inproj(x, row(norm1_g[l]), w_in_b, l)
        sink_tbl = jnp.broadcast_to((attn_sink[l].astype(F32) * LOG2E)[:, None, None], (N_HEADS, 1, LANES))
        attn = _attention(qkv, bias_tbl, sink_tbl, nseq, seq).reshape(n, ATTN_W)
        w_rec = jnp.concatenate([w_rec_a[l], w_rec_x[l]], axis=-1).astype(BF16)
        lru = _lru(xy, conv_w[l].astype(F32), conv_b[l].astype(F32), w_rec, b_rec_a[l].astype(F32),
                   b_rec_x[l].astype(F32), lru_lambda[l].astype(F32), nseq, seq).reshape(n, LRU_W)
        x1, h2, aff_t = _outproj(attn, lru, x, row(grp_g_attn[l]), row(grp_g_lru[l]),
                                 w_out_b, l, row(norm2_g[l]), _split_router(w_router[l]))
        thr, need, gtc, eqc = _thresholds(aff_t, groups)
        meta = _route_meta(gtc, eqc, need, tile_grp, group_first_tile)
        xe, gpos = _dispatch(meta, h2, aff_t, thr, need, tile_group, cap_rows, real_rows)
        ye = _experts(xe, w_gate_b, w_up_b, w_down_b, l, cap_rows)
        if l < depth - 1:
            x = _combine(meta, x1, gpos, ye, cap_rows)
        else:
            y_p, y_s = _combine(meta, x1, gpos, ye, cap_rows, final=(row(final_g), n_p))

    return (y_p.reshape(bp, seq, D_MODEL), y_s.reshape(bs, seq, D_MODEL))
```

```python
import functools

import numpy as np
import jax
import jax.numpy as jnp
from jax import lax
from jax.experimental import pallas as pl
from jax.experimental.pallas import tpu as pltpu

F32 = jnp.float32
BF16 = jnp.bfloat16
I32 = jnp.int32

D_MODEL = 2048
HEAD_DIM = 128
N_HEADS = 8
N_KV_HEADS = 2
GQA_GROUP = N_HEADS // N_KV_HEADS
ATTN_W = N_HEADS * HEAD_DIM
KV_W = N_KV_HEADS * HEAD_DIM
QKV_W = ATTN_W + 2 * KV_W
WINDOW = 128
BLOCK = 128
N_BUCKETS = 32
MAX_DISTANCE = 128
LRU_W = D_MODEL - ATTN_W
LRU_BLOCKS = 8
LRU_BLK = LRU_W // LRU_BLOCKS
CONV_W = 4
LRU_C = 8.0
N_EXPERTS = 16
EC_CAPACITY = 2
D_EXPERT = D_MODEL // 2
EPS = 1e-6
NEG = -1e30
LOG2E = 1.4426950408889634

LANES = 128
SUBLANES = 8
BF16_ROWS = 16
MXU_DIM = 256
VMEM_LIMIT = 56 * 1024 * 1024

ROW_TILE = 512
ATTN_TQ = 2048
ATTN_SKEW = 2
LRU_FINISH = 256
LRU_SEG = 60
ROUTE_TILE = 256
PIECE = BF16_ROWS
EXPERT_TILE = 512
GATE_COLS = LANES
XE_W = D_MODEL + GATE_COLS


def _params(sem, **kw):
    return pltpu.CompilerParams(dimension_semantics=sem, vmem_limit_bytes=VMEM_LIMIT, **kw)


def _rms(x, g):
    return x * lax.rsqrt(jnp.mean(x * x, axis=-1, keepdims=True) + EPS) * g


def _inproj_kernel(x_ref, g_ref, w_ref, qkv_ref, xy_ref):
    h = _rms(x_ref[...], g_ref[...]).astype(BF16)
    qkv_ref[...] = jnp.dot(h, w_ref[:, :QKV_W], preferred_element_type=F32).astype(BF16)
    xy_ref[...] = jnp.dot(h, w_ref[:, QKV_W:], preferred_element_type=F32)


def _layer_weight(w, l):
    return pl.BlockSpec((None,) + w.shape[1:], lambda i: (l, 0, 0), pipeline_mode=pl.Buffered(1))


def _inproj(x, g, w, l):
    n = x.shape[0]
    return pl.pallas_call(
        _inproj_kernel,
        grid=(n // ROW_TILE,),
        in_specs=[
            pl.BlockSpec((ROW_TILE, D_MODEL), lambda i: (i, 0)),
            pl.BlockSpec((1, D_MODEL), lambda i: (0, 0)),
            _layer_weight(w, l),
        ],
        out_specs=[
            pl.BlockSpec((ROW_TILE, QKV_W), lambda i: (i, 0)),
            pl.BlockSpec((ROW_TILE, 2 * LRU_W), lambda i: (i, 0)),
        ],
        out_shape=[
            jax.ShapeDtypeStruct((n, QKV_W), BF16),
            jax.ShapeDtypeStruct((n, 2 * LRU_W), F32),
        ],
        compiler_params=_params(("arbitrary",)),
    )(x, g, w)


def _inproj_first_kernel(xp_ref, xs_ref, g_ref, w_ref, qkv_ref, xy_ref, x_ref, *, tiles_p):
    i = pl.program_id(0)

    def run(src):
        x_ref[...] = src[...]
        _inproj_kernel(src, g_ref, w_ref, qkv_ref, xy_ref)

    pl.when(i < tiles_p)(lambda: run(xp_ref))
    pl.when(i >= tiles_p)(lambda: run(xs_ref))


def _inproj_first(xp, xs, g, w):
    n_p, n_s = xp.shape[0], xs.shape[0]
    n = n_p + n_s
    tile = ROW_TILE // 2
    tiles_p = n_p // tile
    row = lambda w_: pl.BlockSpec((tile, w_), lambda i: (i, 0))
    return pl.pallas_call(
        functools.partial(_inproj_first_kernel, tiles_p=tiles_p),
        grid=(n // tile,),
        in_specs=[
            pl.BlockSpec((tile, D_MODEL), lambda i: (jnp.minimum(i, tiles_p - 1), 0)),
            pl.BlockSpec((tile, D_MODEL), lambda i: (jnp.maximum(i - tiles_p, 0), 0)),
            pl.BlockSpec((1, D_MODEL), lambda i: (0, 0)),
            _layer_weight(w, 0),
        ],
        out_specs=[row(QKV_W), row(2 * LRU_W), row(D_MODEL)],
        out_shape=[
            jax.ShapeDtypeStruct((n, QKV_W), BF16),
            jax.ShapeDtypeStruct((n, 2 * LRU_W), F32),
            jax.ShapeDtypeStruct((n, D_MODEL), F32),
        ],
        compiler_params=_params(("arbitrary",)),
    )(xp, xs, g, w)


def _attn_kernel(q_ref, kp_ref, km_ref, kn_ref, vp_ref, vm_ref, vn_ref, bias_ref, sink_ref, o_ref,
                 s_scr, p_scr, m_scr, d_scr, *, seq):
    i = pl.program_id(2)
    kwin = jnp.concatenate([kp_ref[0], km_ref[0], kn_ref[0]], axis=0)
    vwin = jnp.concatenate([vp_ref[0], vm_ref[0], vn_ref[0]], axis=0)
    qi = lax.broadcasted_iota(I32, (BLOCK, 3 * BLOCK), 0)
    ji = lax.broadcasted_iota(I32, (BLOCK, 3 * BLOCK), 1)
    band = jnp.abs(ji - BLOCK - qi) <= WINDOW
    scale = HEAD_DIM ** -0.5 * LOG2E
    nsb = ATTN_TQ // BLOCK
    unit = lambda sb, g: sb * GQA_GROUP + g
    qrows = lambda sb: slice(sb * BLOCK, (sb + 1) * BLOCK)
    hcols = lambda g: slice(g * HEAD_DIM, (g + 1) * HEAD_DIM)
    sink = [sink_ref[g][:, :1] for g in range(GQA_GROUP)]
    ones = jnp.ones((3 * BLOCK, HEAD_DIM), BF16)
    valid, vext = [], []
    for sb in range(nsb):
        kpos = i * ATTN_TQ + (sb - 1) * BLOCK + ji
        valid.append(band & (kpos >= 0) & (kpos < seq))
        vext.append(jnp.concatenate([vwin[sb * BLOCK:sb * BLOCK + 3 * BLOCK], ones], axis=1))

    def scores(sb, g):
        s = lax.dot_general(q_ref[0, qrows(sb), hcols(g)], kwin[sb * BLOCK:sb * BLOCK + 3 * BLOCK],
                            (((1,), (1,)), ((), ())), preferred_element_type=F32)
        s_scr[unit(sb, g)] = jnp.where(valid[sb], s * scale + bias_ref[g], NEG)

    def row_max(sb, g):
        u = unit(sb, g)
        m_scr[u] = jnp.maximum(jnp.max(s_scr[u], axis=-1, keepdims=True), sink[g])

    def exponent(sb, g):
        u = unit(sb, g)
        p_scr[u] = jnp.exp2(s_scr[u] - m_scr[u]).astype(BF16)

    def values(sb, g):
        u = unit(sb, g)
        o = jnp.dot(p_scr[u], vext[sb], preferred_element_type=F32)
        denom = o[:, HEAD_DIM:HEAD_DIM + 1] + jnp.exp2(sink[g] - m_scr[u])
        o_ref[0, qrows(sb), hcols(g)] = o[:, :HEAD_DIM] / denom

    passes = (scores, row_max, exponent, values)
    units = [(sb, g) for sb in range(nsb) for g in range(GQA_GROUP)]
    for t in range(len(units) + ATTN_SKEW * (len(passes) - 1)):
        for k, fn in enumerate(passes):
            if 0 <= t - ATTN_SKEW * k < len(units):
                fn(*units[t - ATTN_SKEW * k])


def _attention(qkv, bias_tbl, sink_tbl, nseq, seq):
    qkv3 = qkv.reshape(nseq, seq, QKV_W)
    per = ATTN_TQ // BLOCK
    last = seq // BLOCK - 1
    kcol = ATTN_W // HEAD_DIM
    vcol = (ATTN_W + KV_W) // HEAD_DIM

    def side(col, prev):
        if prev:
            return pl.BlockSpec((1, BLOCK, HEAD_DIM), lambda b, k, i: (b, jnp.maximum(i * per - 1, 0), col + k))
        return pl.BlockSpec((1, BLOCK, HEAD_DIM), lambda b, k, i: (b, jnp.minimum(i * per + per, last), col + k))

    def main(col):
        return pl.BlockSpec((1, ATTN_TQ, HEAD_DIM), lambda b, k, i: (b, i, col + k))

    qw = GQA_GROUP * HEAD_DIM
    return pl.pallas_call(
        functools.partial(_attn_kernel, seq=seq),
        grid=(nseq, N_KV_HEADS, seq // ATTN_TQ),
        in_specs=[
            pl.BlockSpec((1, ATTN_TQ, qw), lambda b, k, i: (b, i, k)),
            side(kcol, True), main(kcol), side(kcol, False),
            side(vcol, True), main(vcol), side(vcol, False),
            pl.BlockSpec((GQA_GROUP, BLOCK, 3 * BLOCK), lambda b, k, i: (k, 0, 0)),
            pl.BlockSpec((GQA_GROUP, 1, LANES), lambda b, k, i: (k, 0, 0)),
        ],
        out_specs=pl.BlockSpec((1, ATTN_TQ, qw), lambda b, k, i: (b, i, k)),
        out_shape=jax.ShapeDtypeStruct((nseq, seq, ATTN_W), F32),
        scratch_shapes=[
            pltpu.VMEM((per * GQA_GROUP, BLOCK, 3 * BLOCK), F32),
            pltpu.VMEM((per * GQA_GROUP, BLOCK, 3 * BLOCK), BF16),
            pltpu.VMEM((per * GQA_GROUP, BLOCK, 1), F32),
            pltpu.VMEM((per * GQA_GROUP, BLOCK, 1), F32),
        ],
        compiler_params=_params(("arbitrary", "arbitrary", "arbitrary")),
    )(qkv3, qkv3, qkv3, qkv3, qkv3, qkv3, qkv3, bias_tbl, sink_tbl)


def _t5_bucket(rel):
    nb = N_BUCKETS // 2
    ret = (rel > 0).astype(np.int32) * nb
    n = np.abs(rel)
    max_exact = nb // 2
    large = max_exact + (np.log(np.maximum(n, 1) / max_exact) / np.log(MAX_DISTANCE / max_exact)
                         * (nb - max_exact)).astype(np.int32)
    large = np.minimum(large, nb - 1)
    return (ret + np.where(n < max_exact, n, large)).astype(np.int32)


def _bias_table(rel_bias):
    rel = (np.arange(3 * BLOCK)[None, :] - BLOCK) - np.arange(BLOCK)[:, None]
    onehot = np.eye(N_BUCKETS, dtype=np.float32)[_t5_bucket(rel)]
    return jnp.einsum("qkb,bh->hqk", onehot, rel_bias.astype(F32), precision=lax.Precision.HIGHEST)


def _sublane_scan(a, u, reverse):
    sub = lax.broadcasted_iota(I32, a.shape, 0)
    for d in (1, 2, 4):
        keep = (sub < SUBLANES - d) if reverse else (sub >= d)
        shift = SUBLANES - d if reverse else d
        a_sh = jnp.where(keep, pltpu.roll(a, shift, 0), 1.0)
        u_sh = jnp.where(keep, pltpu.roll(u, shift, 0), 0.0)
        u = a * u_sh + u
        a = a * a_sh
    return a, u


def _lru_chunk(xr_ref, base, seg, d, prm, state):
    cw, cb, w, ba, bx, decay = prm
    halo, carry = state[:3], state[3]
    sub = lax.broadcasted_iota(I32, (SUBLANES, LRU_BLK), 0)
    x = [xr_ref[0, pl.ds(base + j, SUBLANES, stride=seg), :] for j in range(seg)]
    if d == 0:
        edge = [jnp.where(sub == 0, halo[m], pltpu.roll(x[seg - 1 - m], 1, 0)) for m in range(3)]
        at = lambda j: x[j] if j >= 0 else edge[-j - 1]
        taps = lambda j: [at(j - k) for k in range(CONV_W)]
        new_halo = [x[seg - 1 - m][SUBLANES - 1:SUBLANES] for m in range(3)]
    else:
        edge = [jnp.where(sub == SUBLANES - 1, halo[m], pltpu.roll(x[m], SUBLANES - 1, 0)) for m in range(3)]
        at = lambda j: x[j] if j < seg else edge[j - seg]
        taps = lambda j: [at(j + k) for k in range(CONV_W)]
        new_halo = [x[m][0:1] for m in range(3)]
    xc = []
    for j in range(seg):
        t = taps(j)
        acc = cb[d] + cw[d][CONV_W - 1] * t[0]
        for k in range(1, CONV_W):
            acc = acc + cw[d][CONV_W - 1 - k] * t[k]
        xc.append(acc)
    xc_all = jnp.concatenate(xc, axis=0)
    z = jnp.dot(xc_all.astype(BF16), w[d], preferred_element_type=F32)
    r = jax.nn.sigmoid(z[:, :LRU_BLK] + ba[d])
    ig = jax.nn.sigmoid(z[:, LRU_BLK:] + bx[d])
    log_a = decay[d] * r
    a_all = jnp.exp(log_a)
    th = jnp.tanh(log_a)
    u_all = jnp.sqrt(-2.0 * th / (1.0 - th)) * (ig * xc_all)
    a = [a_all[j * SUBLANES:(j + 1) * SUBLANES] for j in range(seg)]
    u = [u_all[j * SUBLANES:(j + 1) * SUBLANES] for j in range(seg)]
    order = list(range(seg)) if d == 0 else list(range(seg - 1, -1, -1))
    hl, pr = [None] * seg, [None] * seg
    prev = None
    for j in order:
        if prev is None:
            hl[j], pr[j] = u[j], a[j]
        else:
            hl[j] = a[j] * hl[prev] + u[j]
            pr[j] = a[j] * pr[prev]
        prev = j
    ai, ui = _sublane_scan(pr[prev], hl[prev], d == 1)
    end = ui + ai * carry
    if d == 0:
        start = jnp.where(sub == 0, carry, pltpu.roll(end, 1, 0))
        new_carry = end[SUBLANES - 1:SUBLANES]
    else:
        start = jnp.where(sub == SUBLANES - 1, carry, pltpu.roll(end, SUBLANES - 1, 0))
        new_carry = end[0:1]
    h = [hl[j] + pr[j] * start for j in range(seg)]
    return h, tuple(new_halo) + (new_carry,)


def _lru_kernel(xr_ref, yr_ref, cw_ref, cb_ref, w_ref, ba_ref, bx_ref, lam_ref, o_ref, hb_ref, *, seq):
    chunk = SUBLANES * LRU_SEG
    nfull = seq // chunk
    tail_seg = (seq - nfull * chunk) // SUBLANES
    prm = (
        [[cw_ref[d, j:j + 1, :] for j in range(CONV_W)] for d in range(2)],
        [cb_ref[d] for d in range(2)],
        [w_ref[d, 0] for d in range(2)],
        [ba_ref[d] for d in range(2)],
        [bx_ref[d] for d in range(2)],
        [-LRU_C * jax.nn.softplus(-lam_ref[d]) for d in range(2)],
    )
    rows = lambda base, j, seg: (0, pl.ds(base + j, SUBLANES, stride=seg), slice(None))

    def scan(dst_ref, base, seg, d, state):
        h, state = _lru_chunk(xr_ref, base, seg, d, prm, state)
        for j in range(seg):
            dst_ref[rows(base, j, seg)] = h[j]
        return state

    def both(k, st):
        fwd = scan(o_ref, k * chunk, LRU_SEG, 0, st[0])
        bwd = scan(hb_ref, (nfull - 1 - k) * chunk, LRU_SEG, 1, st[1])
        return (fwd, bwd)

    zero = jnp.zeros((1, LRU_BLK), F32)
    bwd = (zero,) * 4
    if tail_seg:
        bwd = scan(hb_ref, nfull * chunk, tail_seg, 1, bwd)
    fwd, _ = lax.fori_loop(0, nfull, both, ((zero,) * 4, bwd))
    if tail_seg:
        scan(o_ref, nfull * chunk, tail_seg, 0, fwd)

    def finish(k, _):
        blk = (0, pl.ds(pl.multiple_of(k * LRU_FINISH, LRU_FINISH), LRU_FINISH), slice(None))
        o_ref[blk] = (o_ref[blk] + hb_ref[blk]) * jax.nn.gelu(yr_ref[blk])
        return 0

    lax.fori_loop(0, seq // LRU_FINISH, finish, 0)


def _lru(xy, cw, cb, w, ba, bx, lam, nseq, seq):
    tail = seq % (SUBLANES * LRU_SEG)
    assert tail % SUBLANES == 0 and (tail == 0 or tail // SUBLANES >= CONV_W - 1) and seq % LRU_FINISH == 0
    xy3 = xy.reshape(nseq, seq, 2 * LRU_W)
    vec = lambda: pl.BlockSpec((2, 1, LRU_BLK), lambda b, c: (0, 0, c))
    return pl.pallas_call(
        functools.partial(_lru_kernel, seq=seq),
        grid=(nseq, LRU_BLOCKS),
        in_specs=[
            pl.BlockSpec((1, seq, LRU_BLK), lambda b, c: (b, 0, c)),
            pl.BlockSpec((1, seq, LRU_BLK), lambda b, c: (b, 0, LRU_BLOCKS + c)),
            pl.BlockSpec((2, CONV_W, LRU_BLK), lambda b, c: (0, 0, c)),
            vec(),
            pl.BlockSpec((2, 1, LRU_BLK, 2 * LRU_BLK), lambda b, c: (0, c, 0, 0)),
            vec(), vec(), vec(),
        ],
        out_specs=pl.BlockSpec((1, seq, LRU_BLK), lambda b, c: (b, 0, c)),
        out_shape=jax.ShapeDtypeStruct((nseq, seq, LRU_W), F32),
        scratch_shapes=[pltpu.VMEM((1, seq, LRU_BLK), F32)],
        compiler_params=_params(("arbitrary", "arbitrary")),
    )(xy3, xy3, cw, cb.reshape(2, 1, LRU_W), w, ba.reshape(2, 1, LRU_W),
      bx.reshape(2, 1, LRU_W), lam.reshape(2, 1, LRU_W))


def _outproj_kernel(attn_ref, lru_ref, x_ref, ga_ref, gl_ref, w_ref, g2_ref, wr_ref, x1_ref, h2_ref, aff_ref):
    an = _rms(attn_ref[...], ga_ref[...]).astype(BF16)
    ln = _rms(lru_ref[...], gl_ref[...]).astype(BF16)
    y = jnp.dot(an, w_ref[:ATTN_W], preferred_element_type=F32)
    y = y + jnp.dot(ln, w_ref[ATTN_W:], preferred_element_type=F32)
    x1 = x_ref[...] + y
    x1_ref[...] = x1
    hf = _rms(x1, g2_ref[...])
    h_hi = hf.astype(BF16)
    h2_ref[...] = h_hi
    h_lo = (hf - h_hi.astype(F32)).astype(BF16)
    nt = (((1,), (1,)), ((), ()))
    a = lax.dot_general(wr_ref[...], h_hi, nt, preferred_element_type=F32)
    b = lax.dot_general(wr_ref[:N_EXPERTS], h_lo, nt, preferred_element_type=F32)
    logits = a[:N_EXPERTS] + a[N_EXPERTS:] + b
    e = jnp.exp(logits - jnp.max(logits, axis=0, keepdims=True))
    aff_ref[...] = e / jnp.sum(e, axis=0, keepdims=True)


def _outproj(attn, lru, x, ga, gl, w, l, g2, wr_t):
    n = x.shape[0]
    row = lambda w_: pl.BlockSpec((ROW_TILE, w_), lambda i: (i, 0))
    full = lambda a: pl.BlockSpec(a.shape, lambda i: (0,) * a.ndim)
    return pl.pallas_call(
        _outproj_kernel,
        grid=(n // ROW_TILE,),
        in_specs=[row(ATTN_W), row(LRU_W), row(D_MODEL), full(ga), full(gl),
                  _layer_weight(w, l), full(g2), full(wr_t)],
        out_specs=[row(D_MODEL), row(D_MODEL), pl.BlockSpec((N_EXPERTS, ROW_TILE), lambda i: (0, i))],
        out_shape=[
            jax.ShapeDtypeStruct((n, D_MODEL), F32),
            jax.ShapeDtypeStruct((n, D_MODEL), BF16),
            jax.ShapeDtypeStruct((N_EXPERTS, n), F32),
        ],
        compiler_params=_params(("arbitrary",)),
    )(attn, lru, x, ga, gl, w, g2, wr_t)


def _split_router(w_router):
    wt = w_router.astype(F32).T
    hi = wt.astype(BF16)
    lo = (wt - hi.astype(F32)).astype(BF16)
    return jnp.concatenate([hi, lo], axis=0)


def _threshold_kernel(aff_ref, thr_ref, need_ref, gt_ref, eq_ref, *, groups):
    for gi, (lo, hi, cap) in enumerate(groups):
        def keys():
            return lax.bitcast_convert_type(aff_ref[:, lo:hi], I32)

        def step(it, thr):
            cand = thr | jnp.left_shift(jnp.int32(1), 30 - it)
            cnt = jnp.sum((keys() >= cand).astype(F32), axis=1, keepdims=True)
            return jnp.where(cnt >= cap, cand, thr)

        thr = lax.fori_loop(0, 31, step, jnp.zeros((N_EXPERTS, 1), I32))
        ngt = jnp.sum((keys() > thr).astype(F32), axis=1, keepdims=True)
        thr_ref[gi] = jnp.broadcast_to(thr, (N_EXPERTS, LANES))
        need_ref[gi] = jnp.broadcast_to(cap - ngt, (N_EXPERTS, LANES))
        for t in range(lo // ROUTE_TILE, hi // ROUTE_TILE):
            kt = lax.bitcast_convert_type(aff_ref[:, t * ROUTE_TILE:(t + 1) * ROUTE_TILE], I32)
            gt_ref[t] = jnp.broadcast_to(jnp.sum((kt > thr).astype(F32), axis=1, keepdims=True), (N_EXPERTS, LANES))
            eq_ref[t] = jnp.broadcast_to(jnp.sum((kt == thr).astype(F32), axis=1, keepdims=True), (N_EXPERTS, LANES))


def _thresholds(aff_t, groups):
    n = aff_t.shape[1]
    nt = n // ROUTE_TILE
    ng = len(groups)
    return pl.pallas_call(
        functools.partial(_threshold_kernel, groups=groups),
        out_shape=[
            jax.ShapeDtypeStruct((ng, N_EXPERTS, LANES), I32),
            jax.ShapeDtypeStruct((ng, N_EXPERTS, LANES), F32),
            jax.ShapeDtypeStruct((nt, N_EXPERTS, LANES), F32),
            jax.ShapeDtypeStruct((nt, N_EXPERTS, LANES), F32),
        ],
        compiler_params=pltpu.CompilerParams(vmem_limit_bytes=VMEM_LIMIT),
    )(aff_t)


def _segment_pieces(tile, npiece_s, off_s, fn):
    for e in range(N_EXPERTS):
        npiece = npiece_s[tile * N_EXPERTS + e]
        first = off_s[tile * N_EXPERTS + e]

        def body(j, _, e=e, first=first):
            fn(e, j, pl.multiple_of(first + j * PIECE, PIECE))
            return 0

        lax.fori_loop(0, npiece, body, 0)


def _wait_pieces(piece_copy, count):
    def body(j, _):
        piece_copy.wait()
        return 0

    lax.fori_loop(0, count, body, 0)


def _piece_onehot(gpos, e, row):
    rid = lax.broadcasted_iota(I32, (PIECE, gpos.shape[1]), 0) + row
    return gpos[e:e + 1, :] == rid


def _dispatch_kernel(abase_s, npiece_s, off_s, nch_s, nflush_s, rem_s,
                     h2_ref, aff_ref, thr_ref, need_ref, ceq_ref, offv_ref,
                     xe_ref, gpos_ref, stage, pbuf, carry, zeros, tri, sem, *, tile_group, cap_rows, real_rows):
    i = pl.program_id(0)
    tm = ROUTE_TILE

    @pl.when(i == 0)
    def _():
        pbuf[...] = jnp.zeros(pbuf.shape, BF16)
        earlier = lax.broadcasted_iota(I32, (tm, tm), 0) < lax.broadcasted_iota(I32, (tm, tm), 1)
        tri[...] = jnp.where(earlier, 1.0, 0.0).astype(BF16)

    grp = tile_group(i)
    thr = jnp.where(grp == 0, thr_ref[0], thr_ref[1])[:, :1]
    need = jnp.where(grp == 0, need_ref[0], need_ref[1])[:, :1]
    aff = aff_ref[...]
    keys = lax.bitcast_convert_type(aff, I32)
    gt = keys > thr
    eq = keys == thr
    before = tri[...]
    eq_rank = jnp.dot(jnp.where(eq, 1.0, 0.0).astype(BF16), before, preferred_element_type=F32)
    sel = gt | (eq & (ceq_ref[0][:, :1] + eq_rank < need))
    rank = jnp.dot(jnp.where(sel, 1.0, 0.0).astype(BF16), before, preferred_element_type=F32)
    gpos = jnp.where(sel, offv_ref[0][:, :1] + rank.astype(I32), -1)
    gpos_ref[...] = gpos

    g_hi = aff.astype(BF16).astype(F32)
    g_mid = (aff - g_hi).astype(BF16).astype(F32)
    g_lo = (aff - g_hi - g_mid).astype(BF16).astype(F32)
    pad = jnp.zeros((GATE_COLS - 3 * N_EXPERTS, tm), F32)
    gate_cols = jnp.concatenate([g_hi, g_mid, g_lo, pad], axis=0).T.astype(BF16)

    def build(e, j, row):
        pbuf[pl.ds(row, PIECE), :] = jnp.where(_piece_onehot(gpos, e, row), 1.0, 0.0).astype(BF16)

    _segment_pieces(i, npiece_s, off_s, build)

    nt = pl.num_programs(0)
    slot = lax.rem(i, 2)
    mine = stage.at[slot]

    def chunk(c, _):
        r0 = pl.multiple_of(c * MXU_DIM, MXU_DIM)
        onehot = pbuf[pl.ds(r0, MXU_DIM), :]
        rows = jnp.dot(onehot, h2_ref[...], preferred_element_type=F32)
        mine[pl.ds(r0, MXU_DIM), pl.ds(0, D_MODEL)] = rows.astype(BF16)
        gates = jnp.dot(onehot, gate_cols, preferred_element_type=F32)
        mine[pl.ds(r0, MXU_DIM), pl.ds(D_MODEL, GATE_COLS)] = gates.astype(BF16)
        return 0

    lax.fori_loop(0, nch_s[i], chunk, 0)

    for e in range(N_EXPERTS):
        k = i * N_EXPERTS + e

        @pl.when((npiece_s[k] > 0) & (rem_s[k] > 0))
        def _(e=e, k=k):
            head = pl.ds(pl.multiple_of(off_s[k], PIECE), PIECE)
            mine[head, :] = mine[head, :] + carry[e]

    def copy(s, e, j, row):
        dst = pl.multiple_of(e * cap_rows + abase_s[i * N_EXPERTS + e] + j * PIECE, PIECE)
        return pltpu.make_async_copy(stage.at[s, pl.ds(row, PIECE)], xe_ref.at[pl.ds(dst, PIECE)], sem.at[s])

    _segment_pieces(i, nflush_s, off_s, lambda e, j, row: copy(slot, e, j, row).start())

    for e in range(N_EXPERTS):
        k = i * N_EXPERTS + e

        @pl.when(npiece_s[k] > nflush_s[k])
        def _(e=e, k=k):
            carry[e] = mine[pl.ds(pl.multiple_of(off_s[k] + nflush_s[k] * PIECE, PIECE), PIECE), :]

    @pl.when(i > 0)
    def _():
        _wait_pieces(copy(1 - slot, 0, 0, 0), nch_s[nt + i - 1])

    @pl.when(i == nt - 1)
    def _():
        _wait_pieces(copy(slot, 0, 0, 0), nch_s[nt + i])

    ntail = (cap_rows - real_rows) // PIECE
    if ntail:
        @pl.when(i == nt - 1)
        def _():
            zeros[...] = jnp.zeros(zeros.shape, BF16)
            fills = [pltpu.make_async_copy(zeros, xe_ref.at[pl.ds(e * cap_rows + real_rows + j * PIECE, PIECE)],
                                           sem.at[0])
                     for e in range(N_EXPERTS) for j in range(ntail)]
            for f in fills:
                f.start()
            for f in fills:
                f.wait()


def _dispatch(meta, h2, aff_t, thr, need, tile_group, cap_rows, real_rows):
    n = h2.shape[0]
    nt = n // ROUTE_TILE
    tile3 = lambda: pl.BlockSpec((1, N_EXPERTS, LANES), lambda i, *_: (i, 0, 0))
    full3 = lambda a: pl.BlockSpec(a.shape, lambda i, *_: (0, 0, 0))
    grid_spec = pltpu.PrefetchScalarGridSpec(
        num_scalar_prefetch=6,
        grid=(nt,),
        in_specs=[
            pl.BlockSpec((ROUTE_TILE, D_MODEL), lambda i, *_: (i, 0)),
            pl.BlockSpec((N_EXPERTS, ROUTE_TILE), lambda i, *_: (0, i)),
            full3(thr), full3(need), tile3(), tile3(),
        ],
        out_specs=[
            pl.BlockSpec(memory_space=pl.ANY),
            pl.BlockSpec((N_EXPERTS, ROUTE_TILE), lambda i, *_: (0, i)),
        ],
        scratch_shapes=[
            pltpu.VMEM((2, N_EXPERTS * ROUTE_TILE, XE_W), BF16),
            pltpu.VMEM((N_EXPERTS * ROUTE_TILE, ROUTE_TILE), BF16),
            pltpu.VMEM((N_EXPERTS, PIECE, XE_W), BF16),
            pltpu.VMEM((PIECE, XE_W), BF16),
            pltpu.VMEM((ROUTE_TILE, ROUTE_TILE), BF16),
            pltpu.SemaphoreType.DMA((2,)),
        ],
    )
    return pl.pallas_call(
        functools.partial(_dispatch_kernel, tile_group=tile_group, cap_rows=cap_rows, real_rows=real_rows),
        grid_spec=grid_spec,
        out_shape=[
            jax.ShapeDtypeStruct((N_EXPERTS * cap_rows, XE_W), BF16),
            jax.ShapeDtypeStruct((N_EXPERTS, n), I32),
        ],
        compiler_params=_params(("arbitrary",)),
    )(meta["abase"], meta["npiece"], meta["off"], meta["nch_nflush"], meta["nflush"], meta["rem"],
      h2, aff_t, thr, need, meta["ceq_v"], meta["pos_v"])


def _expert_kernel(xe_ref, wg_ref, wu_ref, wd_ref, ye_ref):
    e = pl.program_id(0)
    x = xe_ref[:, :D_MODEL]
    gc = xe_ref[:, D_MODEL:].astype(F32)
    lane = lax.broadcasted_iota(I32, gc.shape, 1)
    gate = jnp.sum(jnp.where((lane & (N_EXPERTS - 1)) == e, gc, 0.0), axis=1, keepdims=True)
    hg = jnp.dot(x, wg_ref[0], preferred_element_type=F32)
    hu = jnp.dot(x, wu_ref[0], preferred_element_type=F32)
    hid = (jax.nn.silu(hg) * hu).astype(BF16)
    y = jnp.dot(hid, wd_ref[0], preferred_element_type=F32) * gate
    ye_ref[...] = y.astype(BF16)


def _experts(xe, wg, wu, wd, l, cap_rows):
    rt = cap_rows // EXPERT_TILE
    rows = lambda w_: pl.BlockSpec((EXPERT_TILE, w_), lambda e, r: (e * rt + r, 0))
    weight = lambda w: pl.BlockSpec((None, 1) + w.shape[2:], lambda e, r: (l, e, 0, 0))
    return pl.pallas_call(
        _expert_kernel,
        grid=(N_EXPERTS, rt),
        in_specs=[rows(XE_W), weight(wg), weight(wu), weight(wd)],
        out_specs=rows(D_MODEL),
        out_shape=jax.ShapeDtypeStruct((N_EXPERTS * cap_rows, D_MODEL), BF16),
        compiler_params=_params(("arbitrary", "arbitrary")),
    )(xe, wg, wu, wd)


def _combine_kernel(abase_s, npiece_s, off_s, nch_s, rall_s, x1_ref, gpos_ref, ye_ref, x2_ref,
                    stage, pbuf, sem, *, cap_rows):
    i = pl.program_id(0)
    nt = pl.num_programs(0)
    slot = lax.rem(i, 2)

    def copy(t, s, e, j, row):
        src = pl.multiple_of(e * cap_rows + abase_s[t * N_EXPERTS + e] + j * PIECE, PIECE)
        return pltpu.make_async_copy(ye_ref.at[pl.ds(src, PIECE)], stage.at[s, pl.ds(row, PIECE)], sem.at[s])

    def fetch(t, s):
        _segment_pieces(t, npiece_s, off_s, lambda e, j, row: copy(t, s, e, j, row).start())

    @pl.when(i == 0)
    def _():
        stage[...] = jnp.zeros(stage.shape, BF16)
        fetch(0, 0)

    @pl.when(i + 1 < nt)
    def _():
        fetch(i + 1, 1 - slot)

    gpos = gpos_ref[...]

    def build(e, j, row):
        pbuf[pl.ds(row, PIECE), :] = jnp.where(_piece_onehot(gpos, e, row), 1.0, 0.0).astype(BF16)

    _segment_pieces(i, npiece_s, off_s, build)

    def clear(j, _):
        row = pl.multiple_of(rall_s[i] + j * PIECE, PIECE)
        pbuf[pl.ds(row, PIECE), :] = jnp.zeros((PIECE, ROUTE_TILE), BF16)
        return 0

    lax.fori_loop(0, (nch_s[i] * MXU_DIM - rall_s[i]) // PIECE, clear, 0)
    _wait_pieces(copy(i, slot, 0, 0, 0), rall_s[i] // PIECE)
    mine = stage.at[slot]

    def chunk(c):
        r0 = pl.multiple_of(c * MXU_DIM, MXU_DIM)
        onehot_t = pbuf[pl.ds(r0, MXU_DIM), :].astype(F32).T.astype(BF16)
        return jnp.dot(onehot_t, mine[pl.ds(r0, MXU_DIM), :], preferred_element_type=F32)

    @pl.when(nch_s[i] == 0)
    def _():
        x2_ref[...] = x1_ref[...]

    @pl.when(nch_s[i] > 0)
    def _():
        x2_ref[...] = x1_ref[...] + chunk(0)

    def more(c, _):
        x2_ref[...] += chunk(c)
        return 0

    lax.fori_loop(1, nch_s[i], more, 0)


def _combine_final_kernel(abase_s, npiece_s, off_s, nch_s, rall_s, x1_ref, gpos_ref, ye_ref, g_ref,
                          yp_ref, ys_ref, stage, pbuf, sem, acc, *, cap_rows, tiles_p):
    _combine_kernel(abase_s, npiece_s, off_s, nch_s, rall_s, x1_ref, gpos_ref, ye_ref, acc,
                    stage, pbuf, sem, cap_rows=cap_rows)
    i = pl.program_id(0)
    y = _rms(acc[...], g_ref[...])

    @pl.when(i < tiles_p)
    def _():
        yp_ref[...] = y

    @pl.when(i >= tiles_p)
    def _():
        ys_ref[...] = y


def _combine(meta, x1, gpos, ye, cap_rows, final=None):
    n = x1.shape[0]
    tile = lambda: pl.BlockSpec((ROUTE_TILE, D_MODEL), lambda i, *_: (i, 0))
    in_specs = [tile(), pl.BlockSpec((N_EXPERTS, ROUTE_TILE), lambda i, *_: (0, i)),
                pl.BlockSpec(memory_space=pl.ANY)]
    scratch = [
        pltpu.VMEM((2, N_EXPERTS * ROUTE_TILE, D_MODEL), BF16),
        pltpu.VMEM((N_EXPERTS * ROUTE_TILE, ROUTE_TILE), BF16),
        pltpu.SemaphoreType.DMA((2,)),
    ]
    args = (meta["abase"], meta["npiece"], meta["off"], meta["nch"], meta["rall"], x1, gpos, ye)
    if final is None:
        body = functools.partial(_combine_kernel, cap_rows=cap_rows)
        out_specs = tile()
        out_shape = jax.ShapeDtypeStruct((n, D_MODEL), F32)
    else:
        g, n_p = final
        tiles_p = n_p // ROUTE_TILE
        body = functools.partial(_combine_final_kernel, cap_rows=cap_rows, tiles_p=tiles_p)
        in_specs.append(pl.BlockSpec((1, D_MODEL), lambda i, *_: (0, 0)))
        scratch.append(pltpu.VMEM((ROUTE_TILE, D_MODEL), F32))
        out_specs = [
            pl.BlockSpec((ROUTE_TILE, D_MODEL), lambda i, *_: (jnp.minimum(i, tiles_p - 1), 0)),
            pl.BlockSpec((ROUTE_TILE, D_MODEL), lambda i, *_: (jnp.maximum(i - tiles_p, 0), 0)),
        ]
        out_shape = [jax.ShapeDtypeStruct((n_p, D_MODEL), F32), jax.ShapeDtypeStruct((n - n_p, D_MODEL), F32)]
        args = args + (g,)
    grid_spec = pltpu.PrefetchScalarGridSpec(
        num_scalar_prefetch=5, grid=(n // ROUTE_TILE,), in_specs=in_specs, out_specs=out_specs,
        scratch_shapes=scratch)
    return pl.pallas_call(body, grid_spec=grid_spec, out_shape=out_shape,
                          compiler_params=_params(("arbitrary",)))(*args)


def _route_meta(gtc, eqc, need, tile_grp, group_first_tile):
    gtc = gtc[:, :, 0].astype(I32)
    eqc = eqc[:, :, 0].astype(I32)
    need_t = need[:, :, 0].astype(I32)[tile_grp]
    ceq = jnp.cumsum(eqc, axis=0) - eqc
    ceq = ceq - ceq[group_first_tile]
    cnt = gtc + jnp.clip(need_t - ceq, 0, eqc)
    first = jnp.cumsum(cnt, axis=0) - cnt
    rem = first % PIECE
    span = jnp.where(cnt > 0, (rem + cnt + PIECE - 1) // PIECE * PIECE, 0)
    off = jnp.cumsum(span, axis=1) - span
    rall = jnp.sum(span, axis=1)
    nch = (rall + MXU_DIM - 1) // MXU_DIM
    nflush = jnp.where(cnt > 0, (rem + cnt) // PIECE, 0)
    bcast = lambda a, dt: jnp.broadcast_to(a[:, :, None].astype(dt), a.shape + (LANES,))
    return {
        "abase": (first - rem).reshape(-1), "npiece": (span // PIECE).reshape(-1), "off": off.reshape(-1),
        "nflush": nflush.reshape(-1), "rem": rem.reshape(-1), "nch": nch, "rall": rall,
        "nch_nflush": jnp.concatenate([nch, jnp.sum(nflush, axis=1)]),
        "ceq_v": bcast(ceq, F32), "pos_v": bcast(off + rem, I32),
    }


def kernel(x_prompt, x_sample, rel_bias, norm1_g, w_in, attn_sink, conv_w, conv_b, w_rec_a, b_rec_a,
           w_rec_x, b_rec_x, lru_lambda, grp_g_attn, grp_g_lru, w_out, norm2_g, w_router,
           w_e_gate, w_e_up, w_e_down, final_g):
    bp, seq, _ = x_prompt.shape
    bs = x_sample.shape[0]
    assert x_sample.shape[1] == seq
    nseq = bp + bs
    n_p, n_s = bp * seq, bs * seq
    n = n_p + n_s
    assert seq % ATTN_TQ == 0 and n_p % ROW_TILE == 0 and n_s % ROW_TILE == 0
    assert n_p % ROUTE_TILE == 0 and n_s % ROUTE_TILE == 0
    depth = w_in.shape[0]

    groups = ((0, n_p, EC_CAPACITY * n_p // N_EXPERTS), (n_p, n, EC_CAPACITY * n_s // N_EXPERTS))
    nt = n // ROUTE_TILE
    nt_p = n_p // ROUTE_TILE
    tile_grp = np.where(np.arange(nt) < nt_p, 0, 1)
    group_first_tile = np.where(np.arange(nt) < nt_p, 0, nt_p)
    real_rows = groups[0][2] + groups[1][2]
    assert real_rows % PIECE == 0
    cap_rows = -(-real_rows // EXPERT_TILE) * EXPERT_TILE
    tile_group = lambda i: (i >= nt_p).astype(I32)

    bias_tbl = _bias_table(rel_bias) * LOG2E
    row = lambda v: v.reshape(1, -1).astype(F32)
    x = None
    w_in_b, w_out_b = w_in.astype(BF16), w_out.astype(BF16)
    w_gate_b, w_up_b, w_down_b = w_e_gate.astype(BF16), w_e_up.astype(BF16), w_e_down.astype(BF16)

    for l in range(depth):
        if l == 0:
            qkv, xy, x = _inproj_first(x_prompt.reshape(n_p, D_MODEL), x_sample.reshape(n_s, D_MODEL),
                                       row(norm1_g[l]), w_in_b)
        else:
            qkv, xy = _inproj(x, row(norm1_g[l]), w_in_b, l)
        sink_tbl = jnp.broadcast_to((attn_sink[l].astype(F32) * LOG2E)[:, None, None], (N_HEADS, 1, LANES))
        attn = _attention(qkv, bias_tbl, sink_tbl, nseq, seq).reshape(n, ATTN_W)
        w_rec = jnp.concatenate([w_rec_a[l], w_rec_x[l]], axis=-1).astype(BF16)
        lru = _lru(xy, conv_w[l].astype(F32), conv_b[l].astype(F32), w_rec, b_rec_a[l].astype(F32),
                   b_rec_x[l].astype(F32), lru_lambda[l].astype(F32), nseq, seq).reshape(n, LRU_W)
        x1, h2, aff_t = _outproj(attn, lru, x, row(grp_g_attn[l]), row(grp_g_lru[l]),
                                 w_out_b, l, row(norm2_g[l]), _split_router(w_router[l]))
        thr, need, gtc, eqc = _thresholds(aff_t, groups)
        meta = _route_meta(gtc, eqc, need, tile_grp, group_first_tile)
        xe, gpos = _dispatch(meta, h2, aff_t, thr, need, tile_group, cap_rows, real_rows)
        ye = _experts(xe, w_gate_b, w_up_b, w_down_b, l, cap_rows)
        if l < depth - 1:
            x = _combine(meta, x1, gpos, ye, cap_rows)
        else:
            y_p, y_s = _combine(meta, x1, gpos, ye, cap_rows, final=(row(final_g), n_p))

    return (y_p.reshape(bp, seq, D_MODEL), y_s.reshape(bs, seq, D_MODEL))
```
